```python
import math
import jax, jax.numpy as jnp
from jax import lax
import numpy as np

D_MODEL = 1024
BATCH = 2
SEQ = 16384
DEPTH = 4
DEC_BATCH = 4
DEC_SEQ = 8192
PAST_LEN = 128

N_MIXERS = 3
EPS = 1e-6
A_GROUPS = ((128, 1), (512, 4), (2048, 16))
A_N_GROUPS = 3
A_HEADS = 8
A_HEAD_DIM = 128
A_WIDTH = A_HEADS * A_HEAD_DIM
A_IN = 3 * A_N_GROUPS * A_WIDTH + A_WIDTH
ROPE_THETA = 500000.0
ROPE_DIM = A_HEAD_DIM // 4
B_WIDTH = D_MODEL
B_BLOCKS = 8
B_BLOCK_DIM = B_WIDTH // B_BLOCKS
B_CONV = 4
B_C = 8.0
C_HEADS = 8
C_KDIM = 128
C_VDIM = D_MODEL // C_HEADS
C_FWIDTH = C_HEADS * C_KDIM
C_VWIDTH = C_HEADS * C_VDIM
C_CHUNK = 64
C_IN = 3 * C_FWIDTH + 2 * C_VWIDTH

N_A = (DEPTH + 2) // 3
N_B = (DEPTH + 1) // 3
N_C = DEPTH // 3

kernel_name = "hybrid_dilated_attn_rglru_hgrn2_encoder"


def rms_norm(x, g):
    xf = x.astype(jnp.float32)
    y = xf * lax.rsqrt(jnp.mean(xf * xf, axis=-1, keepdims=True) + EPS)
    return (y * g.astype(jnp.float32)).astype(x.dtype)


def partial_rope(x, pos):
    half = ROPE_DIM // 2
    inv = ROPE_THETA ** (-(jnp.arange(half, dtype=jnp.float32) * 2.0) / ROPE_DIM)
    ang = pos[:, None] * inv[None, :]
    cos = jnp.cos(ang)[:, None, None, :]
    sin = jnp.sin(ang)[:, None, None, :]
    xf = x.astype(jnp.float32)
    x1 = xf[..., :half]
    x2 = xf[..., half:ROPE_DIM]
    out = jnp.concatenate([x1 * cos - x2 * sin, x2 * cos + x1 * sin, xf[..., ROPE_DIM:]], axis=-1)
    return out


def dilated_window_attention(q, k, v, window, dilation):
    Bsz, S, H, E = q.shape
    half = window // (2 * dilation)
    blk = half
    L = S // dilation
    n = -(-L // blk)
    Lp = n * blk

    def to_strided(t):
        return t.reshape(Bsz, L, dilation, H, E).transpose(0, 2, 1, 3, 4)

    qs = jnp.pad(to_strided(q), ((0, 0), (0, 0), (0, Lp - L), (0, 0), (0, 0)))
    qs = qs.reshape(Bsz, dilation, n, blk, H, E)
    padk = ((0, 0), (0, 0), (blk, Lp - L + blk), (0, 0), (0, 0))

    def windows(t):
        t = jnp.pad(to_strided(t), padk).reshape(Bsz, dilation, n + 2, blk, H, E)
        return jnp.concatenate([t[:, :, :-2], t[:, :, 1:-1], t[:, :, 2:]], axis=3)

    kw = windows(k)
    vw = windows(v)
    s = jnp.einsum('bdnqhe,bdnkhe->bdnhqk', qs, kw) * (E ** -0.5)
    qi = jnp.arange(n)[:, None, None] * blk + jnp.arange(blk)[None, :, None]
    ki = (jnp.arange(n)[:, None, None] - 1) * blk + jnp.arange(3 * blk)[None, None, :]
    valid = (jnp.abs(qi - ki) <= half) & (ki >= 0) & (ki < L)
    s = jnp.where(valid[:, None], s, -1e30)
    m = jnp.max(s, axis=-1, keepdims=True)
    p = jnp.exp(s - m)
    den = jnp.sum(p, axis=-1, keepdims=True)
    o = jnp.einsum('bdnhqk,bdnkhe->bdnqhe', p, vw) / jnp.swapaxes(den, 3, 4)
    lse = jnp.swapaxes((m + jnp.log(den))[..., 0], 3, 4)
    o = o.reshape(Bsz, dilation, Lp, H, E)[:, :, :L].transpose(0, 2, 1, 3, 4).reshape(Bsz, S, H, E)
    lse = lse.reshape(Bsz, dilation, Lp, H)[:, :, :L].transpose(0, 2, 1, 3).reshape(Bsz, S, H)
    return o, lse


def mixer_a(h, w_in, w_out, pos):
    Bsz, S, _ = h.shape
    proj = h @ w_in
    qkv = proj[..., :3 * A_N_GROUPS * A_WIDTH].reshape(Bsz, S, 3, A_N_GROUPS, A_HEADS, A_HEAD_DIM)
    gate = proj[..., 3 * A_N_GROUPS * A_WIDTH:]
    q = partial_rope(qkv[:, :, 0], pos)
    k = partial_rope(qkv[:, :, 1], pos)
    v = qkv[:, :, 2].astype(jnp.float32)
    outs, lses = [], []
    for g, (window, dil) in enumerate(A_GROUPS):
        o, l = dilated_window_attention(q[:, :, g], k[:, :, g], v[:, :, g], window, dil)
        outs.append(o)
        lses.append(l)
    wts = jax.nn.softmax(jnp.stack(lses, 0), axis=0)
    o = jnp.einsum('gbsh,gbshe->bshe', wts, jnp.stack(outs, 0))
    o = o.reshape(Bsz, S, A_WIDTH).astype(h.dtype) * jax.nn.silu(gate)
    return o @ w_out


def centred_depthwise_conv(x, w, b):
    S = x.shape[1]
    left = B_CONV // 2
    right = B_CONV - 1 - left
    xp = jnp.pad(x, ((0, 0), (left, right), (0, 0)))
    y = xp[:, 0:S] * w[0]
    for t in range(1, B_CONV):
        y = y + xp[:, t:t + S] * w[t]
    return y + b


def linear_combine(left, right):
    a1, b1 = left
    a2, b2 = right
    return a1 * a2, a2 * b1 + b2


def rg_lru(x, w_r, b_r, w_i, b_i, lam, reverse):
    Bsz, S, E = x.shape
    xb = x.reshape(Bsz, S, B_BLOCKS, B_BLOCK_DIM)
    r = jax.nn.sigmoid(jnp.einsum('bsnc,ncd->bsnd', xb, w_r.astype(jnp.float32)).reshape(Bsz, S, E) + b_r)
    i = jax.nn.sigmoid(jnp.einsum('bsnc,ncd->bsnd', xb, w_i.astype(jnp.float32)).reshape(Bsz, S, E) + b_i)
    log_a = -B_C * r * jax.nn.softplus(-lam.astype(jnp.float32))
    a = jnp.exp(log_a)
    u = jnp.sqrt(-jnp.expm1(2.0 * log_a)) * (i * x)
    _, hs = lax.associative_scan(linear_combine, (a, u), reverse=reverse, axis=1)
    return hs


def mixer_b(h, w_in, conv_w, conv_b, w_r, b_r, w_i, b_i, lam, w_out):
    proj = h @ w_in
    xb = proj[..., :B_WIDTH].astype(jnp.float32)
    gate = proj[..., B_WIDTH:]
    xc = centred_depthwise_conv(xb, conv_w.astype(jnp.float32), conv_b.astype(jnp.float32))
    y = (rg_lru(xc, w_r[0], b_r[0], w_i[0], b_i[0], lam[0], False)
         + rg_lru(xc, w_r[1], b_r[1], w_i[1], b_i[1], lam[1], True))
    return (y.astype(h.dtype) * jax.nn.silu(gate)) @ w_out


def hgrn2_chunk_scan(q, k, v, logf):
    Bsz, S, H, K = q.shape
    V = v.shape[-1]
    n = S // C_CHUNK
    q, k, v, logf = [t.reshape(Bsz, n, C_CHUNK, H, t.shape[-1]) for t in (q, k, v, logf)]
    b = jnp.cumsum(logf, axis=2)
    b_last = b[:, :, -1:]
    ref = b[:, :, C_CHUNK // 2:C_CHUNK // 2 + 1]
    att = jnp.einsum('bnthk,bnshk->bnhts', q * jnp.exp(b - ref), k * jnp.exp(ref - b))
    causal = jnp.tril(jnp.ones((C_CHUNK, C_CHUNK), dtype=bool))
    att = jnp.where(causal, att, 0.0)
    o_intra = jnp.einsum('bnhts,bnshv->bnthv', att, v)
    q_dec = q * jnp.exp(b)
    k_dec = k * jnp.exp(b_last - b)
    g = jnp.exp(b_last[:, :, 0])

    def step(state, xs):
        qd, kd, vc, gc = xs
        o = jnp.einsum('bthk,bhkv->bthv', qd, state)
        state = gc[..., None] * state + jnp.einsum('bshk,bshv->bhkv', kd, vc)
        return state, o

    xs = (jnp.moveaxis(q_dec, 1, 0), jnp.moveaxis(k_dec, 1, 0), jnp.moveaxis(v, 1, 0), jnp.moveaxis(g, 1, 0))
    state0 = jnp.zeros((Bsz, H, K, V), jnp.float32)
    _, o_inter = lax.scan(step, state0, xs)
    o = o_intra + jnp.moveaxis(o_inter, 0, 1)
    return o.reshape(Bsz, S, H, V)


def mixer_c(h, w_in, lb, gnorm_g, w_out):
    Bsz, S, _ = h.shape
    proj = h @ w_in
    F = C_FWIDTH
    q = proj[..., :F].astype(jnp.float32).reshape(Bsz, S, C_HEADS, C_KDIM)
    zf = proj[..., F:2 * F]
    zb = proj[..., 2 * F:3 * F]
    vi = proj[..., 3 * F:3 * F + C_VWIDTH].astype(jnp.float32).reshape(Bsz, S, C_HEADS, C_VDIM)
    gate = proj[..., 3 * F + C_VWIDTH:]

    def gates(z, lbd):
        z = z.astype(jnp.float32)
        logf = jnp.logaddexp(jnp.log(lbd), jnp.log1p(-lbd) + jax.nn.log_sigmoid(z))
        kk = (1.0 - lbd) * jax.nn.sigmoid(-z)
        return logf.reshape(Bsz, S, C_HEADS, C_KDIM), kk.reshape(Bsz, S, C_HEADS, C_KDIM)

    lf_f, k_f = gates(zf, lb[0])
    lf_b, k_b = gates(zb, lb[1])
    o_f = hgrn2_chunk_scan(q, k_f, vi, lf_f)
    o_b = jnp.flip(hgrn2_chunk_scan(jnp.flip(q, 1), jnp.flip(k_b, 1), jnp.flip(vi, 1), jnp.flip(lf_b, 1)), 1)
    o = o_f + o_b
    o = o * lax.rsqrt(jnp.mean(o * o, axis=-1, keepdims=True) + EPS) * gnorm_g.astype(jnp.float32)
    o = o.reshape(Bsz, S, C_VWIDTH).astype(h.dtype) * jax.nn.silu(gate)
    return o @ w_out


def trunk(x, norm_g, final_g, a_w_in, a_w_out, b_w_in, b_conv_w, b_conv_b, b_w_r, b_b_r, b_w_i, b_b_i,
          b_lambda, b_w_out, c_w_in, c_lower_bounds, c_gnorm_g, c_w_out):
    S = x.shape[1]
    pos = jnp.arange(S, dtype=jnp.float32)
    sm = jax.nn.softmax(c_lower_bounds.astype(jnp.float32), axis=0)
    lbs = jnp.cumsum(sm, axis=0) - sm[0]
    for layer in range(DEPTH):
        kind = layer % N_MIXERS
        j = layer // N_MIXERS
        h = rms_norm(x, norm_g[layer])
        if kind == 0:
            x = x + mixer_a(h, a_w_in[j], a_w_out[j], pos)
        elif kind == 1:
            x = x + mixer_b(h, b_w_in[j], b_conv_w[j], b_conv_b[j], b_w_r[j], b_b_r[j], b_w_i[j], b_b_i[j],
                            b_lambda[j], b_w_out[j])
        else:
            x = x + mixer_c(h, c_w_in[j], lbs[layer], c_gnorm_g[j], c_w_out[j])
    return rms_norm(x, final_g)


def setup_inputs(seed: int = 0) -> dict:
    key = jax.random.key(seed)
    ks = jax.random.split(key, 20)
    f32 = jnp.float32
    nrm = lambda k, shape, scale: jax.random.normal(k, shape, f32) * scale
    u = jax.random.uniform(ks[12], (N_B, 2, B_WIDTH), f32, 0.9, 0.999) ** (1.0 / B_C)
    return {
        "x_prompt": nrm(ks[0], (BATCH, SEQ, D_MODEL), 1.0),
        "x_sample": nrm(ks[1], (DEC_BATCH, DEC_SEQ, D_MODEL), 1.0),
        "norm_g": 1.0 + nrm(ks[2], (DEPTH, D_MODEL), 0.01),
        "final_g": 1.0 + nrm(ks[3], (D_MODEL,), 0.01),
        "a_w_in": nrm(ks[4], (N_A, D_MODEL, A_IN), D_MODEL ** -0.5),
        "a_w_out": nrm(ks[5], (N_A, A_WIDTH, D_MODEL), A_WIDTH ** -0.5),
        "b_w_in": nrm(ks[6], (N_B, D_MODEL, 2 * B_WIDTH), D_MODEL ** -0.5),
        "b_conv_w": nrm(ks[7], (N_B, B_CONV, B_WIDTH), B_CONV ** -0.5),
        "b_conv_b": nrm(ks[8], (N_B, B_WIDTH), 0.01),
        "b_w_r": nrm(ks[9], (N_B, 2, B_BLOCKS, B_BLOCK_DIM, B_BLOCK_DIM), B_BLOCK_DIM ** -0.5),
        "b_b_r": nrm(ks[10], (N_B, 2, B_WIDTH), 0.01),
        "b_w_i": nrm(ks[11], (N_B, 2, B_BLOCKS, B_BLOCK_DIM, B_BLOCK_DIM), B_BLOCK_DIM ** -0.5),
        "b_b_i": nrm(ks[13], (N_B, 2, B_WIDTH), 0.01),
        "b_lambda": jnp.log(u) - jnp.log1p(-u),
        "b_w_out": nrm(ks[14], (N_B, B_WIDTH, D_MODEL), B_WIDTH ** -0.5),
        "c_w_in": nrm(ks[15], (N_C, D_MODEL, C_IN), D_MODEL ** -0.5),
        "c_lower_bounds": nrm(ks[16], (DEPTH, 2, C_FWIDTH), 0.1),
        "c_gnorm_g": 1.0 + nrm(ks[17], (N_C, C_VDIM), 0.01),
        "c_w_out": nrm(ks[18], (N_C, C_VWIDTH, D_MODEL), C_VWIDTH ** -0.5),
    }


def reference(x_prompt, x_sample, norm_g, final_g, a_w_in, a_w_out, b_w_in, b_conv_w, b_conv_b, b_w_r, b_b_r,
              b_w_i, b_b_i, b_lambda, b_w_out, c_w_in, c_lower_bounds, c_gnorm_g, c_w_out):
    y_prompt = trunk(x_prompt, norm_g, final_g, a_w_in, a_w_out, b_w_in, b_conv_w, b_conv_b, b_w_r, b_b_r,
                     b_w_i, b_b_i, b_lambda, b_w_out, c_w_in, c_lower_bounds, c_gnorm_g, c_w_out)
    y_sample = trunk(x_sample, norm_g, final_g, a_w_in, a_w_out, b_w_in, b_conv_w, b_conv_b, b_w_r, b_b_r,
                     b_w_i, b_b_i, b_lambda, b_w_out, c_w_in, c_lower_bounds, c_gnorm_g, c_w_out)
    return (y_prompt, y_sample)
```

```python
import functools

import jax
import jax.numpy as jnp
from jax import lax
from jax.experimental import pallas as pl
from jax.experimental.pallas import tpu as pltpu

F32 = jnp.float32
BF16 = jnp.bfloat16

D_MODEL = 1024
EPS = 1e-6
LANES = 128
SUBLANES = 8
BF16_ROWS = 16
VMEM_LIMIT = 56 * 1024 * 1024

N_MIXERS = 3
A_GROUPS = ((128, 1), (512, 4), (2048, 16))
A_HEADS = 8
A_HEAD_DIM = 128
A_WIDTH = A_HEADS * A_HEAD_DIM
A_HALF = 64
ROPE_THETA = 500000.0
ROPE_DIM = A_HEAD_DIM // 4
A_QBLK = 128
A_KBLK = A_QBLK + 2 * A_HALF
NEG = -1e30
B_BLOCKS = 8
B_BLOCK_DIM = D_MODEL // B_BLOCKS
B_C = 8.0
C_HEADS = 8
C_KDIM = 128
C_CHUNK = 64


def _params(semantics):
    return pltpu.CompilerParams(dimension_semantics=semantics, vmem_limit_bytes=VMEM_LIMIT)


def _rmsnorm(x, g):
    ms = jnp.mean(x * x, axis=-1, keepdims=True)
    return x * lax.rsqrt(ms + EPS) * g


def _silu(g):
    return g * (1.0 / (1.0 + jnp.exp(-g)))


def _dot(a, b):
    return jnp.dot(a, b, preferred_element_type=F32)


def _dot_nt(a, b):
    return lax.dot_general(a, b, (((1,), (1,)), ((), ())), preferred_element_type=F32)


def _dot_tn(a, b):
    return lax.dot_general(a, b, (((0,), (0,)), ((), ())), preferred_element_type=F32)


def _norm_proj_kernel(*refs, n_chunks, n_rope, q_scale):
    if n_rope:
        x_ref, g_ref, w_ref, ta_ref, tb_ref, tc_ref, o_ref = refs
    else:
        x_ref, g_ref, w_ref, o_ref = refs
    h = _rmsnorm(x_ref[...], g_ref[...]).astype(BF16)
    for c in range(n_chunks):
        acc = _dot(h, w_ref[:, c * D_MODEL:(c + 1) * D_MODEL])
        if c < n_rope:
            ta, tb, tc = ta_ref[...], tb_ref[...], tc_ref[...]
            for hd in range(D_MODEL // LANES):
                s = acc[:, hd * LANES:(hd + 1) * LANES]
                r = (s * ta + pltpu.roll(s, LANES - ROPE_DIM // 2, 1) * tb
                     + pltpu.roll(s, ROPE_DIM // 2, 1) * tc)
                if c == 0 and q_scale is not None:
                    r = r * q_scale
                lo = c * D_MODEL + hd * LANES
                o_ref[:, lo:lo + LANES] = r.astype(o_ref.dtype)
        else:
            o_ref[:, c * D_MODEL:(c + 1) * D_MODEL] = acc.astype(o_ref.dtype)


def _norm_proj(x, g, w, *, dil=1, rope=None, n_rope=0, q_scale=None, tl=512):
    bsz, seq, _ = x.shape
    n = w.shape[1]
    rows = seq // dil
    tl = min(tl, rows)
    xv = x.reshape(bsz, rows, dil * D_MODEL)
    in_specs = [
        pl.BlockSpec((None, tl, D_MODEL), lambda b, r, i: (b, i, r)),
        pl.BlockSpec((1, D_MODEL), lambda b, r, i: (0, 0)),
        pl.BlockSpec((D_MODEL, n), lambda b, r, i: (0, 0)),
    ]
    args = [xv, g.reshape(1, D_MODEL), w]
    if n_rope:
        for t in rope:
            in_specs.append(pl.BlockSpec((tl, LANES), lambda b, r, i: (i, r)))
            args.append(t.reshape(rows, dil * LANES))
    return pl.pallas_call(
        functools.partial(_norm_proj_kernel, n_chunks=n // D_MODEL, n_rope=n_rope, q_scale=q_scale),
        grid=(bsz, dil, rows // tl),
        in_specs=in_specs,
        out_specs=pl.BlockSpec((None, None, tl, n), lambda b, r, i: (b, r, i, 0)),
        out_shape=jax.ShapeDtypeStruct((bsz, dil, rows, n), BF16),
        compiler_params=_params(("arbitrary", "arbitrary", "arbitrary")),
        name="norm_proj",
    )(*args)


def _attn_kernel(*refs, tq, rows, first):
    if first:
        q_ref, k_ref, v_ref, kp_ref, kn_ref, vp_ref, vn_ref, o_ref, lse_ref, kext, vext = refs
    else:
        (q_ref, k_ref, v_ref, kp_ref, kn_ref, vp_ref, vn_ref, op_ref, lp_ref,
         o_ref, lse_ref, kext, vext) = refs
    l0 = pl.program_id(2) * tq
    kext[0:A_HALF, :] = kp_ref[...]
    kext[A_HALF:A_HALF + tq, :] = k_ref[...]
    kext[A_HALF + tq:, :] = kn_ref[...]
    vext[0:A_HALF, :] = vp_ref[...]
    vext[A_HALF:A_HALF + tq, :] = v_ref[...]
    vext[A_HALF + tq:, :] = vn_ref[...]
    lse_ref[...] = jnp.zeros_like(lse_ref)

    qi = lax.broadcasted_iota(jnp.int32, (A_QBLK, A_KBLK), 0)
    ci = lax.broadcasted_iota(jnp.int32, (A_QBLK, A_KBLK), 1)
    rel = ci - qi
    for j in range(tq // A_QBLK):
        key = l0 + (j * A_QBLK - A_HALF) + ci
        bias = jnp.where(rel >= 0, jnp.where(rel <= 2 * A_HALF, 0.0, NEG), NEG)
        bias = jnp.where(key >= 0, jnp.where(key < rows, bias, NEG), NEG)
        r0 = j * A_QBLK
        for hd in range(A_HEADS):
            c0 = hd * A_HEAD_DIM
            q = q_ref[r0:r0 + A_QBLK, c0:c0 + A_HEAD_DIM]
            k = kext[r0:r0 + A_KBLK, c0:c0 + A_HEAD_DIM]
            v = vext[r0:r0 + A_KBLK, c0:c0 + A_HEAD_DIM]
            s = _dot_nt(q, k) + bias
            m = jnp.max(s, axis=1, keepdims=True)
            p = jnp.exp(s - m)
            den = jnp.sum(p, axis=1, keepdims=True)
            o = _dot(p.astype(BF16), v) / den
            lse = m + jnp.log(den)
            if not first:
                lp = lp_ref[r0:r0 + A_QBLK, hd:hd + 1]
                mx = jnp.maximum(lp, lse)
                wp = jnp.exp(lp - mx)
                wo = jnp.exp(lse - mx)
                tot = wp + wo
                o = (op_ref[r0:r0 + A_QBLK, c0:c0 + A_HEAD_DIM] * wp + o * wo) / tot
                lse = mx + jnp.log(tot)
            o_ref[r0:r0 + A_QBLK, c0:c0 + A_HEAD_DIM] = o
            lse_ref[r0:r0 + A_QBLK, hd:hd + 1] = lse


def _attn_group(qkv, prev, *, dil, tq=512):
    bsz, _, rows, _ = qkv.shape
    seq = rows * dil
    tq = min(tq, rows)
    nh = tq // A_HALF
    last = rows // A_HALF - 1
    first = prev is None

    def main(col):
        return pl.BlockSpec((None, None, tq, A_WIDTH), lambda b, r, i: (b, r, i, col))

    def halo_prev(col):
        return pl.BlockSpec((None, None, A_HALF, A_WIDTH),
                            lambda b, r, i: (b, r, jnp.maximum(i * nh - 1, 0), col))

    def halo_next(col):
        return pl.BlockSpec((None, None, A_HALF, A_WIDTH),
                            lambda b, r, i: (b, r, jnp.minimum((i + 1) * nh, last), col))

    o_spec = pl.BlockSpec((None, tq, A_WIDTH), lambda b, r, i: (b, i, r))
    l_spec = pl.BlockSpec((None, tq, LANES), lambda b, r, i: (b, i, r))
    in_specs = [main(0), main(1), main(2), halo_prev(1), halo_next(1), halo_prev(2), halo_next(2)]
    args = [qkv] * 7
    if not first:
        in_specs += [o_spec, l_spec]
        args += [prev[0].reshape(bsz, rows, dil * A_WIDTH), prev[1].reshape(bsz, rows, dil * LANES)]
    o, lse = pl.pallas_call(
        functools.partial(_attn_kernel, tq=tq, rows=rows, first=first),
        grid=(bsz, dil, rows // tq),
        in_specs=in_specs,
        out_specs=[o_spec, l_spec],
        out_shape=[jax.ShapeDtypeStruct((bsz, rows, dil * A_WIDTH), F32),
                   jax.ShapeDtypeStruct((bsz, rows, dil * LANES), F32)],
        scratch_shapes=[pltpu.VMEM((tq + 2 * A_HALF, A_WIDTH), BF16),
                        pltpu.VMEM((tq + 2 * A_HALF, A_WIDTH), BF16)],
        compiler_params=_params(("arbitrary", "arbitrary", "arbitrary")),
        name="attn_group",
    )(*args)
    return o.reshape(bsz, seq, A_WIDTH), lse.reshape(bsz, seq, LANES)


def _a_out_kernel(*refs, final):
    if final:
        x_ref, o_ref, g_ref, wg_ref, wo_ref, fg_ref, y_ref = refs
    else:
        x_ref, o_ref, g_ref, wg_ref, wo_ref, y_ref = refs
    x = x_ref[...]
    h = _rmsnorm(x, g_ref[...]).astype(BF16)
    gate = _dot(h, wg_ref[...])
    z = (o_ref[...] * _silu(gate)).astype(BF16)
    y = x + _dot(z, wo_ref[...])
    if final:
        y = _rmsnorm(y, fg_ref[...])
    y_ref[...] = y


def _a_out(x, o, g, w_gate, w_out, final_g, *, tm=512):
    bsz, seq, _ = x.shape
    tm = min(tm, seq)
    final = final_g is not None
    tile = pl.BlockSpec((None, tm, D_MODEL), lambda b, i: (b, i, 0))
    vec = pl.BlockSpec((1, D_MODEL), lambda b, i: (0, 0))
    mat = pl.BlockSpec((D_MODEL, D_MODEL), lambda b, i: (0, 0))
    in_specs = [tile, tile, vec, mat, mat]
    args = [x, o, g.reshape(1, D_MODEL), w_gate, w_out]
    if final:
        in_specs.append(vec)
        args.append(final_g.reshape(1, D_MODEL))
    return pl.pallas_call(
        functools.partial(_a_out_kernel, final=final),
        grid=(bsz, seq // tm),
        in_specs=in_specs,
        out_specs=tile,
        out_shape=jax.ShapeDtypeStruct((bsz, seq, D_MODEL), F32),
        compiler_params=_params(("arbitrary", "arbitrary")),
        name="a_out",
    )(*args)


def _rope_tables(seq):
    half = ROPE_DIM // 2
    pos = jnp.arange(seq, dtype=F32)
    inv = ROPE_THETA ** (-(jnp.arange(half, dtype=F32) * 2.0) / ROPE_DIM)
    ang = pos[:, None] * inv[None, :]
    cos, sin = jnp.cos(ang), jnp.sin(ang)
    zeros = jnp.zeros((seq, LANES - ROPE_DIM), F32)
    zh = jnp.zeros((seq, half), F32)
    ta = jnp.concatenate([cos, cos, jnp.ones((seq, LANES - ROPE_DIM), F32)], axis=1)
    tb = jnp.concatenate([-sin, zh, zeros], axis=1)
    tc = jnp.concatenate([zh, sin, zeros], axis=1)
    return ta, tb, tc


def _mixer_a(x, g, w_in, w_out, final_g):
    seq = x.shape[1]
    rope = _rope_tables(seq)
    n_qkv = 3 * len(A_GROUPS) * A_WIDTH
    prev = None
    for grp in reversed(range(len(A_GROUPS))):
        dil = A_GROUPS[grp][1]
        cols = [w_in[:, t * len(A_GROUPS) * A_WIDTH + grp * A_WIDTH:][:, :A_WIDTH] for t in range(3)]
        w_g = jnp.concatenate(cols, axis=1).astype(BF16)
        qkv = _norm_proj(x, g, w_g, dil=dil, rope=rope, n_rope=2, q_scale=A_HEAD_DIM ** -0.5)
        prev = _attn_group(qkv, prev, dil=dil)
    return _a_out(x, prev[0], g, w_in[:, n_qkv:].astype(BF16), w_out.astype(BF16), final_g)


def _rglru_kernel(*refs, tile, n_tiles, reverse):
    if reverse:
        (xm_ref, xp_ref, xn_ref, cw_ref, cb_ref, wg_ref, br_ref, bi_ref, lam_ref,
         hf_ref, gate_ref, x_ref, wo_ref, out_ref, a_scr, u_scr, h_scr) = refs
    else:
        (xm_ref, xp_ref, xn_ref, cw_ref, cb_ref, wg_ref, br_ref, bi_ref, lam_ref,
         out_ref, a_scr, u_scr, h_scr) = refs
    step = pl.program_id(1)
    t = (n_tiles - 1 - step) if reverse else step

    xm = xm_ref[...].astype(F32)
    prev = xp_ref[...].astype(F32)
    nxt = xn_ref[...].astype(F32)
    has_prev = t > 0
    has_next = t < n_tiles - 1
    p2 = jnp.where(has_prev, prev[BF16_ROWS - 2:BF16_ROWS - 1, :], 0.0)
    p1 = jnp.where(has_prev, prev[BF16_ROWS - 1:BF16_ROWS, :], 0.0)
    n0 = jnp.where(has_next, nxt[0:1, :], 0.0)
    row = lax.broadcasted_iota(jnp.int32, (tile, D_MODEL), 0)
    xm1 = jnp.where(row == 0, p1, pltpu.roll(xm, 1, 0))
    xm2 = jnp.where(row == 0, p2, jnp.where(row == 1, p1, pltpu.roll(xm, 2, 0)))
    xp1 = jnp.where(row == tile - 1, n0, pltpu.roll(xm, tile - 1, 0))
    cw = cw_ref[...]
    xc = xm2 * cw[0:1, :] + xm1 * cw[1:2, :] + xm * cw[2:3, :] + xp1 * cw[3:4, :] + cb_ref[...]

    lam = lam_ref[...]
    softplus = jnp.log1p(jnp.exp(-jnp.abs(lam))) + jnp.maximum(-lam, 0.0)
    for n in range(B_BLOCKS):
        c0 = n * B_BLOCK_DIM
        xcn = xc[:, c0:c0 + B_BLOCK_DIM]
        gates = _dot(xcn.astype(BF16), wg_ref[n])
        r = 1.0 / (1.0 + jnp.exp(-(gates[:, :B_BLOCK_DIM] + br_ref[:, c0:c0 + B_BLOCK_DIM])))
        ig = 1.0 / (1.0 + jnp.exp(-(gates[:, B_BLOCK_DIM:] + bi_ref[:, c0:c0 + B_BLOCK_DIM])))
        log_a = -B_C * r * softplus[:, c0:c0 + B_BLOCK_DIM]
        a = jnp.exp(log_a)
        mult = jnp.sqrt(-jnp.tanh(log_a) * (a * a + 1.0))
        a_scr[:, c0:c0 + B_BLOCK_DIM] = a
        u_scr[:, c0:c0 + B_BLOCK_DIM] = mult * (ig * xcn)

    @pl.when(step == 0)
    def _():
        h_scr[...] = jnp.zeros_like(h_scr)

    n_groups = tile // SUBLANES
    row8 = lax.broadcasted_iota(jnp.int32, (SUBLANES, D_MODEL), 0)

    def body(gi, h):
        g = (n_groups - 1 - gi) if reverse else gi
        off = pl.multiple_of(g * SUBLANES, SUBLANES)
        a8 = a_scr[pl.ds(off, SUBLANES), :]
        u8 = u_scr[pl.ds(off, SUBLANES), :]
        for s in (1, 2, 4):
            if reverse:
                a_sh = pltpu.roll(a8, SUBLANES - s, 0)
                u_sh = pltpu.roll(u8, SUBLANES - s, 0)
                keep = row8 < SUBLANES - s
            else:
                a_sh = pltpu.roll(a8, s, 0)
                u_sh = pltpu.roll(u8, s, 0)
                keep = row8 >= s
            u8 = jnp.where(keep, a8 * u_sh + u8, u8)
            a8 = jnp.where(keep, a8 * a_sh, a8)
        hs = a8 * h + u8
        u_scr[pl.ds(off, SUBLANES), :] = hs
        return hs[0:1, :] if reverse else hs[SUBLANES - 1:SUBLANES, :]

    h_scr[...] = lax.fori_loop(0, n_groups, body, h_scr[...])

    if reverse:
        y = hf_ref[...] + u_scr[...]
        z = (y * _silu(gate_ref[...].astype(F32))).astype(BF16)
        out_ref[...] = x_ref[...] + _dot(z, wo_ref[...])
    else:
        out_ref[...] = u_scr[...]


def _rglru(proj, conv_w, conv_b, wg, b_r, b_i, lam, *, reverse, hf=None, x=None, w_out=None, tile=512):
    bsz, seq, _ = proj.shape
    tile = min(tile, seq)
    n_tiles = seq // tile
    per = tile // BF16_ROWS
    last = seq // BF16_ROWS - 1

    def tix(i):
        return (n_tiles - 1 - i) if reverse else i

    main = pl.BlockSpec((None, tile, D_MODEL), lambda b, i: (b, tix(i), 0))
    halo_p = pl.BlockSpec((None, BF16_ROWS, D_MODEL), lambda b, i: (b, jnp.maximum(tix(i) * per - 1, 0), 0))
    halo_n = pl.BlockSpec((None, BF16_ROWS, D_MODEL), lambda b, i: (b, jnp.minimum((tix(i) + 1) * per, last), 0))
    vec = pl.BlockSpec((1, D_MODEL), lambda b, i: (0, 0))
    in_specs = [main, halo_p, halo_n,
                pl.BlockSpec((4, D_MODEL), lambda b, i: (0, 0)), vec,
                pl.BlockSpec((B_BLOCKS, B_BLOCK_DIM, 2 * B_BLOCK_DIM), lambda b, i: (0, 0, 0)),
                vec, vec, vec]
    args = [proj, proj, proj, conv_w, conv_b.reshape(1, D_MODEL), wg,
            b_r.reshape(1, D_MODEL), b_i.reshape(1, D_MODEL), lam.reshape(1, D_MODEL)]
    if reverse:
        in_specs += [main, pl.BlockSpec((None, tile, D_MODEL), lambda b, i: (b, tix(i), 1)), main,
                     pl.BlockSpec((D_MODEL, D_MODEL), lambda b, i: (0, 0))]
        args += [hf, proj, x, w_out]
    return pl.pallas_call(
        functools.partial(_rglru_kernel, tile=tile, n_tiles=n_tiles, reverse=reverse),
        grid=(bsz, n_tiles),
        in_specs=in_specs,
        out_specs=main,
        out_shape=jax.ShapeDtypeStruct((bsz, seq, D_MODEL), F32),
        scratch_shapes=[pltpu.VMEM((tile, D_MODEL), F32), pltpu.VMEM((tile, D_MODEL), F32),
                        pltpu.VMEM((1, D_MODEL), F32)],
        compiler_params=_params(("arbitrary", "arbitrary")),
        name="rglru_bwd_out" if reverse else "rglru_fwd",
    )(*args)


def _mixer_b(x, g, w_in, conv_w, conv_b, w_r, b_r, w_i, b_i, lam, w_out):
    bsz, seq, _ = x.shape
    proj = _norm_proj(x, g, w_in.astype(BF16)).reshape(bsz, seq, 2 * D_MODEL)
    wg = [jnp.concatenate([w_r[d], w_i[d]], axis=-1).astype(BF16) for d in range(2)]
    hf = _rglru(proj, conv_w, conv_b, wg[0], b_r[0], b_i[0], lam[0], reverse=False)
    return _rglru(proj, conv_w, conv_b, wg[1], b_r[1], b_i[1], lam[1], reverse=True,
                  hf=hf, x=x, w_out=w_out.astype(BF16))


def _split3(x):
    hi = x.astype(BF16)
    r1 = x - hi.astype(F32)
    mid = r1.astype(BF16)
    lo = (r1 - mid.astype(F32)).astype(BF16)
    return hi, mid, lo


def _hgrn_kernel(*refs, tile, reverse):
    if reverse:
        (q_ref, z_ref, v_ref, lb_ref, tri_ref, of_ref, gate_ref, x_ref, gn_ref, wo_ref,
         out_ref, st_scr, o_scr) = refs
    else:
        q_ref, z_ref, v_ref, lb_ref, tri_ref, out_ref, st_scr, o_scr = refs

    @pl.when(pl.program_id(1) == 0)
    def _():
        st_scr[...] = jnp.zeros_like(st_scr)

    lb = lb_ref[...]
    one_m_lb = 1.0 - lb
    tri = tri_ref[...]
    ti = lax.broadcasted_iota(jnp.int32, (C_CHUNK, C_CHUNK), 0)
    si = lax.broadcasted_iota(jnp.int32, (C_CHUNK, C_CHUNK), 1)
    visible = (si >= ti) if reverse else (si <= ti)
    mid_row = C_CHUNK // 2 - 1 if reverse else C_CHUNK // 2
    end_row = 0 if reverse else C_CHUNK - 1
    n_chunks = tile // C_CHUNK
    for cc in range(n_chunks):
        c = (n_chunks - 1 - cc) if reverse else cc
        r0 = c * C_CHUNK
        z = z_ref[r0:r0 + C_CHUNK, :].astype(F32)
        e = jnp.exp(-jnp.abs(z))
        inv = 1.0 / (1.0 + e)
        sig_pos = jnp.where(z >= 0, inv, e * inv)
        sig_neg = jnp.where(z >= 0, e * inv, inv)
        logf = jnp.log(lb + one_m_lb * sig_pos)
        kk = one_m_lb * sig_neg
        hi, mid, lo = _split3(logf)
        b = _dot(tri, hi) + _dot(tri, mid) + _dot(tri, lo)
        b_mid = b[mid_row:mid_row + 1, :]
        b_end = b[end_row:end_row + 1, :]
        qt = q_ref[r0:r0 + C_CHUNK, :].astype(F32) * jnp.exp(b - b_mid)
        kt = kk * jnp.exp(b_mid - b)
        qd = (qt * jnp.exp(b_mid)).astype(BF16)
        kd = (kt * jnp.exp(b_end - b_mid)).astype(BF16)
        qt = qt.astype(BF16)
        kt = kt.astype(BF16)
        decay = jnp.exp(b_end)
        for hd in range(C_HEADS):
            c0 = hd * C_KDIM
            v = v_ref[r0:r0 + C_CHUNK, c0:c0 + C_KDIM]
            att = _dot_nt(qt[:, c0:c0 + C_KDIM], kt[:, c0:c0 + C_KDIM])
            att = jnp.where(visible, att, 0.0).astype(BF16)
            st = st_scr[hd]
            o = _dot(att, v) + _dot_nt(qd[:, c0:c0 + C_KDIM], st.astype(BF16))
            st_scr[hd] = decay[:, c0:c0 + C_KDIM] * st + _dot_tn(v, kd[:, c0:c0 + C_KDIM])
            o_scr[r0:r0 + C_CHUNK, c0:c0 + C_KDIM] = o

    if reverse:
        gn = gn_ref[...]
        for hd in range(C_HEADS):
            c0 = hd * C_KDIM
            o = of_ref[:, c0:c0 + C_KDIM] + o_scr[:, c0:c0 + C_KDIM]
            o = o * lax.rsqrt(jnp.mean(o * o, axis=-1, keepdims=True) + EPS) * gn
            o_scr[:, c0:c0 + C_KDIM] = o
        z = (o_scr[...] * _silu(gate_ref[...].astype(F32))).astype(BF16)
        out_ref[...] = x_ref[...] + _dot(z, wo_ref[...])
    else:
        out_ref[...] = o_scr[...]


def _hgrn(proj, lb, *, reverse, of=None, x=None, gn=None, w_out=None, tile=256):
    bsz, seq, _ = proj.shape
    tile = min(tile, seq)
    n_tiles = seq // tile

    def tix(i):
        return (n_tiles - 1 - i) if reverse else i

    def col(j):
        return pl.BlockSpec((None, tile, D_MODEL), lambda b, i: (b, tix(i), j))

    ti = lax.broadcasted_iota(jnp.int32, (C_CHUNK, C_CHUNK), 0)
    si = lax.broadcasted_iota(jnp.int32, (C_CHUNK, C_CHUNK), 1)
    tri = ((si >= ti) if reverse else (si <= ti)).astype(BF16)
    in_specs = [col(0), col(2 if reverse else 1), col(3),
                pl.BlockSpec((1, D_MODEL), lambda b, i: (0, 0)),
                pl.BlockSpec((C_CHUNK, C_CHUNK), lambda b, i: (0, 0))]
    args = [proj, proj, proj, lb.reshape(1, D_MODEL), tri]
    if reverse:
        in_specs += [col(0), col(4), col(0),
                     pl.BlockSpec((1, C_KDIM), lambda b, i: (0, 0)),
                     pl.BlockSpec((D_MODEL, D_MODEL), lambda b, i: (0, 0))]
        args += [of, proj, x, gn.reshape(1, C_KDIM), w_out]
    return pl.pallas_call(
        functools.partial(_hgrn_kernel, tile=tile, reverse=reverse),
        grid=(bsz, n_tiles),
        in_specs=in_specs,
        out_specs=col(0),
        out_shape=jax.ShapeDtypeStruct((bsz, seq, D_MODEL), F32),
        scratch_shapes=[pltpu.VMEM((C_HEADS, C_KDIM, C_KDIM), F32), pltpu.VMEM((tile, D_MODEL), F32)],
        compiler_params=_params(("arbitrary", "arbitrary")),
        name="hgrn_bwd_out" if reverse else "hgrn_fwd",
    )(*args)


def _mixer_c(x, g, w_in, lb, gn, w_out):
    bsz, seq, _ = x.shape
    proj = _norm_proj(x, g, w_in.astype(BF16)).reshape(bsz, seq, 5 * D_MODEL)
    of = _hgrn(proj, lb[0], reverse=False)
    return _hgrn(proj, lb[1], reverse=True, of=of, x=x, gn=gn, w_out=w_out.astype(BF16))


def _trunk(x, norm_g, final_g, a_w_in, a_w_out, b_w_in, b_conv_w, b_conv_b, b_w_r, b_b_r, b_w_i, b_b_i,
           b_lambda, b_w_out, c_w_in, c_lower_bounds, c_gnorm_g, c_w_out):
    depth = norm_g.shape[0]
    sm = jax.nn.softmax(c_lower_bounds.astype(F32), axis=0)
    lbs = jnp.cumsum(sm, axis=0) - sm[0]
    for layer in range(depth):
        kind = layer % N_MIXERS
        j = layer // N_MIXERS
        fg = final_g if layer == depth - 1 else None
        if kind == 0:
            x = _mixer_a(x, norm_g[layer], a_w_in[j], a_w_out[j], fg)
        elif kind == 1:
            x = _mixer_b(x, norm_g[layer], b_w_in[j], b_conv_w[j], b_conv_b[j], b_w_r[j], b_b_r[j],
                         b_w_i[j], b_b_i[j], b_lambda[j], b_w_out[j])
        else:
            x = _mixer_c(x, norm_g[layer], c_w_in[j], lbs[layer], c_gnorm_g[j], c_w_out[j])
        if fg is not None and kind != 0:
            raise NotImplementedError("final norm is fused into the attention output kernel")
    return x


def kernel(x_prompt, x_sample, norm_g, final_g, a_w_in, a_w_out, b_w_in, b_conv_w, b_conv_b, b_w_r, b_b_r,
           b_w_i, b_b_i, b_lambda, b_w_out, c_w_in, c_lower_bounds, c_gnorm_g, c_w_out):
    weights = (norm_g, final_g, a_w_in, a_w_out, b_w_in, b_conv_w, b_conv_b, b_w_r, b_b_r, b_w_i, b_b_i,
               b_lambda, b_w_out, c_w_in, c_lower_bounds, c_gnorm_g, c_w_out)
    return (_trunk(x_prompt, *weights), _trunk(x_sample, *weights))
```

```python
import functools

import numpy as np
import jax
import jax.numpy as jnp
from jax import lax
from jax.experimental import pallas as pl
from jax.experimental.pallas import tpu as pltpu

F32 = jnp.float32
BF16 = jnp.bfloat16

D_MODEL = 1024
EPS = 1e-6
LANES = 128
SUBLANES = 8
BF16_ROWS = 16
VMEM_LIMIT = 56 * 1024 * 1024

N_MIXERS = 3
A_GROUPS = ((128, 1), (512, 4), (2048, 16))
A_HEADS = 8
A_HEAD_DIM = 128
A_WIDTH = A_HEADS * A_HEAD_DIM
A_HALF = 64
ROPE_THETA = 500000.0
ROPE_DIM = A_HEAD_DIM // 4
A_QBLK = 128
A_KBLK = A_QBLK + 2 * A_HALF
A_PERM = 256
NEG = -1e30
B_BLOCKS = 8
B_BLOCK_DIM = D_MODEL // B_BLOCKS
B_C = 8.0
C_HEADS = 8
C_KDIM = 128
C_CHUNK = 64


def _params(semantics):
    return pltpu.CompilerParams(dimension_semantics=semantics, vmem_limit_bytes=VMEM_LIMIT)


def _rmsnorm(x, g):
    ms = jnp.mean(x * x, axis=-1, keepdims=True)
    return x * lax.rsqrt(ms + EPS) * g


def _sigmoid(x):
    return 0.5 * jnp.tanh(0.5 * x) + 0.5


def _silu(g):
    return g * _sigmoid(g)


def _dot(a, b):
    return jnp.dot(a, b, preferred_element_type=F32)


def _dot_nt(a, b):
    return lax.dot_general(a, b, (((1,), (1,)), ((), ())), preferred_element_type=F32)


def _dot_tn(a, b):
    return lax.dot_general(a, b, (((0,), (0,)), ((), ())), preferred_element_type=F32)


def _split2(x):
    hi = x.astype(BF16)
    return hi, (x - hi.astype(F32)).astype(BF16)


def _deinterleave_matrix(dil):
    n = A_PERM // dil
    idx = np.arange(A_PERM)
    p = np.zeros((A_PERM, A_PERM), np.float32)
    p[idx, (idx % n) * dil + idx // n] = 1.0
    return p


def _norm_proj_kernel(*refs, n_chunks, n_rope, q_scale, dil, tl):
    refs = list(refs)
    x_ref, g_ref, w_ref = refs[:3]
    o_ref = refs[-1]
    p_ref = refs[3] if dil > 1 else None
    tabs = refs[-4:-1] if n_rope else None
    n_sub = tl // A_PERM
    n = A_PERM // dil

    h = _rmsnorm(x_ref[...], g_ref[...]).astype(BF16)
    if dil > 1:
        p = p_ref[...]
        h = jnp.concatenate(
            [_dot(p, h[s * A_PERM:(s + 1) * A_PERM, :]).astype(BF16) for s in range(n_sub)], axis=0)

    def store(val, lo, width):
        val = val.astype(o_ref.dtype)
        if dil == 1:
            o_ref[0, :, lo:lo + width] = val
        else:
            for s in range(n_sub):
                for r in range(dil):
                    a = s * A_PERM + r * n
                    o_ref[r, s * n:(s + 1) * n, lo:lo + width] = val[a:a + n, :]

    for c in range(n_chunks):
        acc = _dot(h, w_ref[:, c * D_MODEL:(c + 1) * D_MODEL])
        if c < n_rope:
            ta, tb, tc = tabs[0][...], tabs[1][...], tabs[2][...]
            for hd in range(D_MODEL // LANES):
                s = acc[:, hd * LANES:(hd + 1) * LANES]
                r = (s * ta + pltpu.roll(s, LANES - ROPE_DIM // 2, 1) * tb
                     + pltpu.roll(s, ROPE_DIM // 2, 1) * tc)
                if c == 0 and q_scale is not None:
                    r = r * q_scale
                store(r, c * D_MODEL + hd * LANES, LANES)
        else:
            store(acc, c * D_MODEL, D_MODEL)


def _norm_proj(x, g, w, *, dil=1, rope=None, n_rope=0, q_scale=None, tl=512):
    bsz, seq, _ = x.shape
    n = w.shape[1]
    tl = min(tl, seq)
    full = lambda b, i: (0, 0)
    in_specs = [
        pl.BlockSpec((None, tl, D_MODEL), lambda b, i: (b, i, 0)),
        pl.BlockSpec((1, D_MODEL), full),
        pl.BlockSpec((D_MODEL, n), full),
    ]
    args = [x, g.reshape(1, D_MODEL), w]
    if dil > 1:
        in_specs.append(pl.BlockSpec((A_PERM, A_PERM), full))
        args.append(jnp.asarray(_deinterleave_matrix(dil), BF16))
    if n_rope:
        for t in rope:
            in_specs.append(pl.BlockSpec((tl, LANES), lambda b, i: (i, 0)))
            args.append(t)
    return pl.pallas_call(
        functools.partial(_norm_proj_kernel, n_chunks=n // D_MODEL, n_rope=n_rope, q_scale=q_scale,
                          dil=dil, tl=tl),
        grid=(bsz, seq // tl),
        in_specs=in_specs,
        out_specs=pl.BlockSpec((None, dil, tl // dil, n), lambda b, i: (b, 0, i, 0)),
        out_shape=jax.ShapeDtypeStruct((bsz, dil, seq // dil, n), BF16),
        compiler_params=_params(("arbitrary", "arbitrary")),
        name="norm_proj",
    )(*args)


def _attn_kernel(q_ref, k_ref, v_ref, kp_ref, kn_ref, vp_ref, vn_ref, o_ref, lse_ref, kext, vext, *, tq, rows):
    l0 = pl.program_id(2) * tq
    first = (pl.program_id(0) == 0) & (pl.program_id(1) == 0) & (pl.program_id(2) == 0)

    @pl.when(first)
    def _():
        vext[...] = jnp.ones_like(vext)

    kext[0:A_HALF, :] = kp_ref[...]
    kext[A_HALF:A_HALF + tq, :] = k_ref[...]
    kext[A_HALF + tq:, :] = kn_ref[...]
    for hd in range(A_HEADS):
        src = slice(hd * A_HEAD_DIM, (hd + 1) * A_HEAD_DIM)
        dst = slice(2 * hd * A_HEAD_DIM, (2 * hd + 1) * A_HEAD_DIM)
        vext[0:A_HALF, dst] = vp_ref[:, src]
        vext[A_HALF:A_HALF + tq, dst] = v_ref[:, src]
        vext[A_HALF + tq:, dst] = vn_ref[:, src]

    qi = lax.broadcasted_iota(jnp.int32, (A_QBLK, A_KBLK), 0)
    ci = lax.broadcasted_iota(jnp.int32, (A_QBLK, A_KBLK), 1)
    lane = lax.broadcasted_iota(jnp.int32, (A_QBLK, LANES), 1)
    rel = ci - qi
    for j in range(tq // A_QBLK):
        key = l0 + (j * A_QBLK - A_HALF) + ci
        bias = jnp.where(rel >= 0, jnp.where(rel <= 2 * A_HALF, 0.0, NEG), NEG)
        bias = jnp.where(key >= 0, jnp.where(key < rows, bias, NEG), NEG)
        r0 = j * A_QBLK
        lse_tile = jnp.zeros((A_QBLK, LANES), F32)
        for hd in range(A_HEADS):
            c0 = hd * A_HEAD_DIM
            q = q_ref[r0:r0 + A_QBLK, c0:c0 + A_HEAD_DIM]
            k = kext[r0:r0 + A_KBLK, c0:c0 + A_HEAD_DIM]
            v1 = vext[r0:r0 + A_KBLK, 2 * c0:2 * c0 + 2 * A_HEAD_DIM]
            s = _dot_nt(q, k) + bias
            m = jnp.max(s, axis=1, keepdims=True)
            p = jnp.exp(s - m).astype(BF16)
            acc = _dot(p, v1)
            den = acc[:, A_HEAD_DIM:]
            o_ref[r0:r0 + A_QBLK, c0:c0 + A_HEAD_DIM] = (acc[:, :A_HEAD_DIM] / den).astype(o_ref.dtype)
            lse_tile = jnp.where(lane == hd, m + jnp.log(den), lse_tile)
        lse_ref[r0:r0 + A_QBLK, :] = lse_tile


def _attn_group(qkv, *, tq=512):
    bsz, dil, rows, _ = qkv.shape
    tq = min(tq, rows)
    nh = tq // A_HALF
    last = rows // A_HALF - 1

    def main(col, width=A_WIDTH):
        return pl.BlockSpec((None, None, tq, width), lambda b, r, i: (b, r, i, col))

    def halo_prev(col):
        return pl.BlockSpec((None, None, A_HALF, A_WIDTH),
                            lambda b, r, i: (b, r, jnp.maximum(i * nh - 1, 0), col))

    def halo_next(col):
        return pl.BlockSpec((None, None, A_HALF, A_WIDTH),
                            lambda b, r, i: (b, r, jnp.minimum((i + 1) * nh, last), col))

    return pl.pallas_call(
        functools.partial(_attn_kernel, tq=tq, rows=rows),
        grid=(bsz, dil, rows // tq),
        in_specs=[main(0), main(1), main(2), halo_prev(1), halo_next(1), halo_prev(2), halo_next(2)],
        out_specs=[main(0), main(0, LANES)],
        out_shape=[jax.ShapeDtypeStruct((bsz, dil, rows, A_WIDTH), BF16),
                   jax.ShapeDtypeStruct((bsz, dil, rows, LANES), F32)],
        scratch_shapes=[pltpu.VMEM((tq + 2 * A_HALF, A_WIDTH), BF16),
                        pltpu.VMEM((tq + 2 * A_HALF, 2 * A_WIDTH), BF16)],
        compiler_params=_params(("arbitrary", "arbitrary", "arbitrary")),
        name="attn_group",
    )(qkv, qkv, qkv, qkv, qkv, qkv, qkv)


def _a_out_kernel(*refs, final, dils):
    n_g = len(dils)
    x_ref = refs[0]
    o_refs = refs[1:1 + n_g]
    l_refs = refs[1 + n_g:1 + 2 * n_g]
    n_perm = sum(1 for d in dils if d > 1)
    p_refs = refs[1 + 2 * n_g:1 + 2 * n_g + n_perm]
    g_ref, wg_ref, wo_ref = refs[1 + 2 * n_g + n_perm:4 + 2 * n_g + n_perm]
    fg_ref = refs[-2] if final else None
    y_ref = refs[-1]

    outs, lses = [], []
    pi = 0
    for gi, dil in enumerate(dils):
        o = o_refs[gi][...].reshape(A_PERM, A_WIDTH)
        lse = l_refs[gi][...].reshape(A_PERM, LANES)
        if dil > 1:
            pt = p_refs[pi][...]
            pi += 1
            o = _dot(pt, o)
            hi, lo = _split2(lse)
            lse = _dot(pt, hi) + _dot(pt, lo)
        else:
            o = o.astype(F32)
        outs.append(o)
        lses.append(lse)

    mx = functools.reduce(jnp.maximum, lses)
    ws = [jnp.exp(l - mx) for l in lses]
    inv = 1.0 / functools.reduce(lambda a, b: a + b, ws)
    ws = [w * inv for w in ws]

    x = x_ref[...]
    h = _rmsnorm(x, g_ref[...]).astype(BF16)
    gate = _silu(_dot(h, wg_ref[...]))
    cols = []
    for hd in range(A_HEADS):
        c0 = hd * A_HEAD_DIM
        o = outs[0][:, c0:c0 + A_HEAD_DIM] * ws[0][:, hd:hd + 1]
        for gi in range(1, n_g):
            o = o + outs[gi][:, c0:c0 + A_HEAD_DIM] * ws[gi][:, hd:hd + 1]
        cols.append((o * gate[:, c0:c0 + A_HEAD_DIM]).astype(BF16))
    y = x + _dot(jnp.concatenate(cols, axis=1), wo_ref[...])
    if final:
        y = _rmsnorm(y, fg_ref[...])
    y_ref[...] = y


def _a_out(x, outs, lses, g, w_gate, w_out, final_g):
    bsz, seq, _ = x.shape
    tm = A_PERM
    final = final_g is not None
    dils = tuple(o.shape[1] for o in outs)
    tile = pl.BlockSpec((None, tm, D_MODEL), lambda b, i: (b, i, 0))
    vec = pl.BlockSpec((1, D_MODEL), lambda b, i: (0, 0))
    mat = pl.BlockSpec((D_MODEL, D_MODEL), lambda b, i: (0, 0))
    in_specs = [tile]
    args = [x]
    for width, arrs in ((A_WIDTH, outs), (LANES, lses)):
        for d, a in zip(dils, arrs):
            in_specs.append(pl.BlockSpec((None, d, tm // d, width), lambda b, i: (b, 0, i, 0)))
            args.append(a)
    for d in dils:
        if d > 1:
            in_specs.append(pl.BlockSpec((A_PERM, A_PERM), lambda b, i: (0, 0)))
            args.append(jnp.asarray(_deinterleave_matrix(d).T, BF16))
    in_specs += [vec, mat, mat]
    args += [g.reshape(1, D_MODEL), w_gate, w_out]
    if final:
        in_specs.append(vec)
        args.append(final_g.reshape(1, D_MODEL))
    return pl.pallas_call(
        functools.partial(_a_out_kernel, final=final, dils=dils),
        grid=(bsz, seq // tm),
        in_specs=in_specs,
        out_specs=tile,
        out_shape=jax.ShapeDtypeStruct((bsz, seq, D_MODEL), F32),
        compiler_params=_params(("arbitrary", "arbitrary")),
        name="a_out",
    )(*args)


def _rope_tables(seq, dil):
    half = ROPE_DIM // 2
    pos = jnp.arange(seq, dtype=F32)
    inv = ROPE_THETA ** (-(jnp.arange(half, dtype=F32) * 2.0) / ROPE_DIM)
    ang = pos[:, None] * inv[None, :]
    cos, sin = jnp.cos(ang), jnp.sin(ang)
    zeros = jnp.zeros((seq, LANES - ROPE_DIM), F32)
    zh = jnp.zeros((seq, half), F32)
    ta = jnp.concatenate([cos, cos, jnp.ones((seq, LANES - ROPE_DIM), F32)], axis=1)
    tb = jnp.concatenate([-sin, zh, zeros], axis=1)
    tc = jnp.concatenate([zh, sin, zeros], axis=1)

    def order(t):
        if dil == 1:
            return t
        t = t.reshape(seq // A_PERM, A_PERM // dil, dil, LANES)
        return jnp.swapaxes(t, 1, 2).reshape(seq, LANES)

    return order(ta), order(tb), order(tc)


def _mixer_a(x, g, w_in, w_out, final_g):
    seq = x.shape[1]
    n_qkv = 3 * len(A_GROUPS) * A_WIDTH
    outs, lses = [], []
    for grp, (_, dil) in enumerate(A_GROUPS):
        cols = [w_in[:, t * len(A_GROUPS) * A_WIDTH + grp * A_WIDTH:][:, :A_WIDTH] for t in range(3)]
        w_g = jnp.concatenate(cols, axis=1).astype(BF16)
        qkv = _norm_proj(x, g, w_g, dil=dil, rope=_rope_tables(seq, dil), n_rope=2,
                         q_scale=A_HEAD_DIM ** -0.5)
        o, lse = _attn_group(qkv)
        outs.append(o)
        lses.append(lse)
    return _a_out(x, outs, lses, g, w_in[:, n_qkv:].astype(BF16), w_out.astype(BF16), final_g)


def _rglru_kernel(*refs, tile, n_tiles, reverse):
    if reverse:
        (xm_ref, xp_ref, xn_ref, cw_ref, cb_ref, wg_ref, br_ref, bi_ref, lam_ref,
         hf_ref, gate_ref, x_ref, wo_ref, out_ref, a_scr, u_scr, h_scr) = refs
    else:
        (xm_ref, xp_ref, xn_ref, cw_ref, cb_ref, wg_ref, br_ref, bi_ref, lam_ref,
         out_ref, a_scr, u_scr, h_scr) = refs
    step = pl.program_id(1)
    t = (n_tiles - 1 - step) if reverse else step

    prev = jnp.where(t > 0, xp_ref[...].astype(F32), 0.0)
    nxt = jnp.where(t < n_tiles - 1, xn_ref[...].astype(F32), 0.0)
    xm = xm_ref[...].astype(F32)
    ext = jnp.concatenate([prev, xm, nxt], axis=0)
    n_ext = tile + 2 * BF16_ROWS
    xm1 = pltpu.roll(ext, 1, 0)[BF16_ROWS:BF16_ROWS + tile, :]
    xm2 = pltpu.roll(ext, 2, 0)[BF16_ROWS:BF16_ROWS + tile, :]
    xp1 = pltpu.roll(ext, n_ext - 1, 0)[BF16_ROWS:BF16_ROWS + tile, :]
    cw = cw_ref[...]
    xc = xm2 * cw[0:1, :] + xm1 * cw[1:2, :] + xm * cw[2:3, :] + xp1 * cw[3:4, :] + cb_ref[...]

    lam = lam_ref[...]
    softplus = jnp.log1p(jnp.exp(-jnp.abs(lam))) + jnp.maximum(-lam, 0.0)
    for n in range(B_BLOCKS):
        c0 = n * B_BLOCK_DIM
        xcn = xc[:, c0:c0 + B_BLOCK_DIM]
        gates = _dot(xcn.astype(BF16), wg_ref[n])
        r = _sigmoid(gates[:, :B_BLOCK_DIM] + br_ref[:, c0:c0 + B_BLOCK_DIM])
        ig = _sigmoid(gates[:, B_BLOCK_DIM:] + bi_ref[:, c0:c0 + B_BLOCK_DIM])
        log_a = -B_C * r * softplus[:, c0:c0 + B_BLOCK_DIM]
        a = jnp.exp(log_a)
        mult = jnp.sqrt(-jnp.tanh(log_a) * (a * a + 1.0))
        a_scr[:, c0:c0 + B_BLOCK_DIM] = a
        u_scr[:, c0:c0 + B_BLOCK_DIM] = mult * (ig * xcn)

    @pl.when(step == 0)
    def _():
        h_scr[...] = jnp.zeros_like(h_scr)

    n_groups = tile // SUBLANES
    row8 = lax.broadcasted_iota(jnp.int32, (SUBLANES, D_MODEL), 0)

    def body(gi, h):
        g = (n_groups - 1 - gi) if reverse else gi
        off = pl.multiple_of(g * SUBLANES, SUBLANES)
        a8 = a_scr[pl.ds(off, SUBLANES), :]
        u8 = u_scr[pl.ds(off, SUBLANES), :]
        for s in (1, 2, 4):
            if reverse:
                a_sh = pltpu.roll(a8, SUBLANES - s, 0)
                u_sh = pltpu.roll(u8, SUBLANES - s, 0)
                keep = row8 < SUBLANES - s
            else:
                a_sh = pltpu.roll(a8, s, 0)
                u_sh = pltpu.roll(u8, s, 0)
                keep = row8 >= s
            u8 = jnp.where(keep, a8 * u_sh + u8, u8)
            a8 = jnp.where(keep, a8 * a_sh, a8)
        hs = a8 * h + u8
        u_scr[pl.ds(off, SUBLANES), :] = hs
        return hs[0:1, :] if reverse else hs[SUBLANES - 1:SUBLANES, :]

    h_scr[...] = lax.fori_loop(0, n_groups, body, h_scr[...], unroll=4)

    if reverse:
        y = hf_ref[...] + u_scr[...]
        z = (y * _silu(gate_ref[...].astype(F32))).astype(BF16)
        out_ref[...] = x_ref[...] + _dot(z, wo_ref[...])
    else:
        out_ref[...] = u_scr[...]


def _rglru(proj, conv_w, conv_b, wg, b_r, b_i, lam, *, reverse, hf=None, x=None, w_out=None, tile=512):
    bsz, seq, _ = proj.shape
    tile = min(tile, seq)
    n_tiles = seq // tile
    per = tile // BF16_ROWS
    last = seq // BF16_ROWS - 1

    def tix(i):
        return (n_tiles - 1 - i) if reverse else i

    main = pl.BlockSpec((None, tile, D_MODEL), lambda b, i: (b, tix(i), 0))
    halo_p = pl.BlockSpec((None, BF16_ROWS, D_MODEL), lambda b, i: (b, jnp.maximum(tix(i) * per - 1, 0), 0))
    halo_n = pl.BlockSpec((None, BF16_ROWS, D_MODEL), lambda b, i: (b, jnp.minimum((tix(i) + 1) * per, last), 0))
    vec = pl.BlockSpec((1, D_MODEL), lambda b, i: (0, 0))
    in_specs = [main, halo_p, halo_n,
                pl.BlockSpec((4, D_MODEL), lambda b, i: (0, 0)), vec,
                pl.BlockSpec((B_BLOCKS, B_BLOCK_DIM, 2 * B_BLOCK_DIM), lambda b, i: (0, 0, 0)),
                vec, vec, vec]
    args = [proj, proj, proj, conv_w, conv_b.reshape(1, D_MODEL), wg,
            b_r.reshape(1, D_MODEL), b_i.reshape(1, D_MODEL), lam.reshape(1, D_MODEL)]
    if reverse:
        in_specs += [main, pl.BlockSpec((None, tile, D_MODEL), lambda b, i: (b, tix(i), 1)), main,
                     pl.BlockSpec((D_MODEL, D_MODEL), lambda b, i: (0, 0))]
        args += [hf, proj, x, w_out]
    return pl.pallas_call(
        functools.partial(_rglru_kernel, tile=tile, n_tiles=n_tiles, reverse=reverse),
        grid=(bsz, n_tiles),
        in_specs=in_specs,
        out_specs=main,
        out_shape=jax.ShapeDtypeStruct((bsz, seq, D_MODEL), F32),
        scratch_shapes=[pltpu.VMEM((tile, D_MODEL), F32), pltpu.VMEM((tile, D_MODEL), F32),
                        pltpu.VMEM((1, D_MODEL), F32)],
        compiler_params=_params(("arbitrary", "arbitrary")),
        name="rglru_bwd_out" if reverse else "rglru_fwd",
    )(*args)


def _mixer_b(x, g, w_in, conv_w, conv_b, w_r, b_r, w_i, b_i, lam, w_out):
    bsz, seq, _ = x.shape
    proj = _norm_proj(x, g, w_in.astype(BF16)).reshape(bsz, seq, 2 * D_MODEL)
    wg = [jnp.concatenate([w_r[d], w_i[d]], axis=-1).astype(BF16) for d in range(2)]
    hf = _rglru(proj, conv_w, conv_b, wg[0], b_r[0], b_i[0], lam[0], reverse=False)
    return _rglru(proj, conv_w, conv_b, wg[1], b_r[1], b_i[1], lam[1], reverse=True,
                  hf=hf, x=x, w_out=w_out.astype(BF16))


def _split3(x):
    hi = x.astype(BF16)
    r1 = x - hi.astype(F32)
    mid = r1.astype(BF16)
    lo = (r1 - mid.astype(F32)).astype(BF16)
    return hi, mid, lo


def _hgrn_kernel(*refs, tile, reverse):
    if reverse:
        (q_ref, z_ref, v_ref, lb_ref, tri_ref, of_ref, gate_ref, x_ref, gn_ref, wo_ref,
         out_ref, st_scr, o_scr) = refs
    else:
        q_ref, z_ref, v_ref, lb_ref, tri_ref, out_ref, st_scr, o_scr = refs

    @pl.when(pl.program_id(1) == 0)
    def _():
        st_scr[...] = jnp.zeros_like(st_scr)

    n_blk = tile // C_CHUNK
    order = list(reversed(range(n_blk))) if reverse else list(range(n_blk))
    mid_row = C_CHUNK // 2 - 1 if reverse else C_CHUNK // 2
    end_row = 0 if reverse else C_CHUNK - 1
    lb = lb_ref[...]
    one_m_lb = 1.0 - lb
    tri = tri_ref[...]

    qt, kt, rho, blk_total = {}, {}, {}, {}
    for j in order:
        r0 = j * C_CHUNK
        z = z_ref[r0:r0 + C_CHUNK, :].astype(F32)
        e = jnp.exp(-jnp.abs(z))
        inv = 1.0 / (1.0 + e)
        sig_pos = jnp.where(z >= 0, inv, e * inv)
        sig_neg = jnp.where(z >= 0, e * inv, inv)
        logf = jnp.log(lb + one_m_lb * sig_pos)
        b = _dot(tri, jnp.concatenate(_split3(logf), axis=0))
        b_mid = b[mid_row:mid_row + 1, :]
        qt[j] = q_ref[r0:r0 + C_CHUNK, :].astype(F32) * jnp.exp(b - b_mid)
        kt[j] = (one_m_lb * sig_neg) * jnp.exp(b_mid - b)
        rho[j] = b_mid
        blk_total[j] = b[end_row:end_row + 1, :]

    off = jnp.zeros_like(lb)
    for j in order:
        rho[j] = rho[j] + off
        off = off + blk_total[j]
    b_end = off
    decay = jnp.exp(b_end)

    qd = jnp.concatenate([(qt[j] * jnp.exp(rho[j])).astype(BF16) for j in range(n_blk)], axis=0)
    kd = jnp.concatenate([(kt[j] * jnp.exp(b_end - rho[j])).astype(BF16) for j in range(n_blk)], axis=0)

    zeros = jnp.zeros((C_CHUNK, D_MODEL), BF16)
    keys = {}
    for i in range(n_blk):
        parts = []
        for j in range(n_blk):
            if j == i:
                parts.append(kt[j].astype(BF16))
            elif order.index(j) < order.index(i):
                parts.append((kt[j] * jnp.exp(rho[i] - rho[j])).astype(BF16))
            else:
                parts.append(zeros)
        keys[i] = jnp.concatenate(parts, axis=0)
    qt = {j: qt[j].astype(BF16) for j in qt}

    ti = lax.broadcasted_iota(jnp.int32, (tile, tile), 0)
    si = lax.broadcasted_iota(jnp.int32, (tile, tile), 1)
    visible = (si >= ti) if reverse else (si <= ti)
    zq = jnp.zeros((C_CHUNK, C_KDIM), BF16)
    for hd in range(C_HEADS):
        cs = slice(hd * C_KDIM, (hd + 1) * C_KDIM)
        q_diag = jnp.concatenate(
            [jnp.concatenate([qt[i][:, cs] if j == i else zq for j in range(n_blk)], axis=1)
             for i in range(n_blk)], axis=0)
        k_cat = jnp.concatenate([keys[i][:, cs] for i in range(n_blk)], axis=1)
        att = jnp.where(visible, _dot_nt(q_diag, k_cat), 0.0).astype(BF16)
        st = st_scr[hd]
        o_scr[:, cs] = _dot(att, v_ref[:, cs]) + _dot_nt(qd[:, cs], st.astype(BF16))
        st_scr[hd] = decay[:, cs] * st + _dot_tn(v_ref[:, cs], kd[:, cs])

    if reverse:
        gn = gn_ref[...]
        for hd in range(C_HEADS):
            c0 = hd * C_KDIM
            o = of_ref[:, c0:c0 + C_KDIM] + o_scr[:, c0:c0 + C_KDIM]
            o = o * lax.rsqrt(jnp.mean(o * o, axis=-1, keepdims=True) + EPS) * gn
            o_scr[:, c0:c0 + C_KDIM] = o
        z = (o_scr[...] * _silu(gate_ref[...].astype(F32))).astype(BF16)
        out_ref[...] = x_ref[...] + _dot(z, wo_ref[...])
    else:
        out_ref[...] = o_scr[...]


def _hgrn(proj, lb, *, reverse, of=None, x=None, gn=None, w_out=None, tile=256):
    bsz, seq, _ = proj.shape
    tile = min(tile, seq)
    n_tiles = seq // tile

    def tix(i):
        return (n_tiles - 1 - i) if reverse else i

    def col(j):
        return pl.BlockSpec((None, tile, D_MODEL), lambda b, i: (b, tix(i), j))

    ti = lax.broadcasted_iota(jnp.int32, (C_CHUNK, C_CHUNK), 0)
    si = lax.broadcasted_iota(jnp.int32, (C_CHUNK, C_CHUNK), 1)
    tri = ((si >= ti) if reverse else (si <= ti)).astype(BF16)
    tri = jnp.concatenate([tri, tri, tri], axis=1)
    in_specs = [col(0), col(2 if reverse else 1), col(3),
                pl.BlockSpec((1, D_MODEL), lambda b, i: (0, 0)),
                pl.BlockSpec((C_CHUNK, 3 * C_CHUNK), lambda b, i: (0, 0))]
    args = [proj, proj, proj, lb.reshape(1, D_MODEL), tri]
    if reverse:
        in_specs += [col(0), col(4), col(0),
                     pl.BlockSpec((1, C_KDIM), lambda b, i: (0, 0)),
                     pl.BlockSpec((D_MODEL, D_MODEL), lambda b, i: (0, 0))]
        args += [of, proj, x, gn.reshape(1, C_KDIM), w_out]
    return pl.pallas_call(
        functools.partial(_hgrn_kernel, tile=tile, reverse=reverse),
        grid=(bsz, n_tiles),
        in_specs=in_specs,
        out_specs=col(0),
        out_shape=jax.ShapeDtypeStruct((bsz, seq, D_MODEL), F32),
        scratch_shapes=[pltpu.VMEM((C_HEADS, C_KDIM, C_KDIM), F32), pltpu.VMEM((tile, D_MODEL), F32)],
        compiler_params=_params(("arbitrary", "arbitrary")),
        name="hgrn_bwd_out" if reverse else "hgrn_fwd",
    )(*args)


def _mixer_c(x, g, w_in, lb, gn, w_out):
    bsz, seq, _ = x.shape
    proj = _norm_proj(x, g, w_in.astype(BF16)).reshape(bsz, seq, 5 * D_MODEL)
    of = _hgrn(proj, lb[0], reverse=False)
    return _hgrn(proj, lb[1], reverse=True, of=of, x=x, gn=gn, w_out=w_out.astype(BF16))


def _trunk(x, norm_g, final_g, a_w_in, a_w_out, b_w_in, b_conv_w, b_conv_b, b_w_r, b_b_r, b_w_i, b_b_i,
           b_lambda, b_w_out, c_w_in, c_lower_bounds, c_gnorm_g, c_w_out):
    depth = norm_g.shape[0]
    sm = jax.nn.softmax(c_lower_bounds.astype(F32), axis=0)
    lbs = jnp.cumsum(sm, axis=0) - sm[0]
    for layer in range(depth):
        kind = layer % N_MIXERS
        j = layer // N_MIXERS
        fg = final_g if layer == depth - 1 else None
        if kind == 0:
            x = _mixer_a(x, norm_g[layer], a_w_in[j], a_w_out[j], fg)
        elif kind == 1:
            x = _mixer_b(x, norm_g[layer], b_w_in[j], b_conv_w[j], b_conv_b[j], b_w_r[j], b_b_r[j],
                         b_w_i[j], b_b_i[j], b_lambda[j], b_w_out[j])
        else:
            x = _mixer_c(x, norm_g[layer], c_w_in[j], lbs[layer], c_gnorm_g[j], c_w_out[j])
        if fg is not None and kind != 0:
            raise NotImplementedError("final norm is fused into the attention output kernel")
    return x


def kernel(x_prompt, x_sample, norm_g, final_g, a_w_in, a_w_out, b_w_in, b_conv_w, b_conv_b, b_w_r, b_b_r,
           b_w_i, b_b_i, b_lambda, b_w_out, c_w_in, c_lower_bounds, c_gnorm_g, c_w_out):
    weights = (norm_g, final_g, a_w_in, a_w_out, b_w_in, b_conv_w, b_conv_b, b_w_r, b_b_r, b_w_i, b_b_i,
               b_lambda, b_w_out, c_w_in, c_lower_bounds, c_gnorm_g, c_w_out)
    return (_trunk(x_prompt, *weights), _trunk(x_sample, *weights))
```

```python
import functools

import numpy as np
import jax
import jax.numpy as jnp
from jax import lax
from jax.experimental import pallas as pl
from jax.experimental.pallas import tpu as pltpu

F32 = jnp.float32
BF16 = jnp.bfloat16

D_MODEL = 1024
EPS = 1e-6
LANES = 128
SUBLANES = 8
BF16_ROWS = 16
VMEM_LIMIT = 56 * 1024 * 1024

N_MIXERS = 3
A_GROUPS = ((128, 1), (512, 4), (2048, 16))
A_HEADS = 8
A_HEAD_DIM = 128
A_WIDTH = A_HEADS * A_HEAD_DIM
A_HALF = 64
ROPE_THETA = 500000.0
ROPE_DIM = A_HEAD_DIM // 4
A_QBLK = 128
A_KBLK = A_QBLK + 2 * A_HALF
A_PERM = 256
PROJ_PIECE = 256
NEG = -1e30
LN2 = 0.6931471805599453
B_BLOCKS = 8
B_BLOCK_DIM = D_MODEL // B_BLOCKS
B_C = 8.0
C_HEADS = 8
C_KDIM = 128
C_CHUNK = 64
C_TILE = 256


def _params(semantics):
    return pltpu.CompilerParams(dimension_semantics=semantics, vmem_limit_bytes=VMEM_LIMIT)


def _rmsnorm(x, g):
    ms = jnp.mean(x * x, axis=-1, keepdims=True)
    return x * lax.rsqrt(ms + EPS) * g


def _sigmoid(x):
    return 0.5 * jnp.tanh(0.5 * x) + 0.5


def _silu(g):
    return g * _sigmoid(g)


def _dot(a, b):
    return jnp.dot(a, b, preferred_element_type=F32)


def _dot_nt(a, b):
    return lax.dot_general(a, b, (((1,), (1,)), ((), ())), preferred_element_type=F32)


def _dot_tn(a, b):
    return lax.dot_general(a, b, (((0,), (0,)), ((), ())), preferred_element_type=F32)


def _split2(x):
    hi = x.astype(BF16)
    return hi, (x - hi.astype(F32)).astype(BF16)


def _deinterleave_matrix(dil):
    n = A_PERM // dil
    idx = np.arange(A_PERM)
    p = np.zeros((A_PERM, A_PERM), np.float32)
    p[idx, (idx % n) * dil + idx // n] = 1.0
    return p


def _norm_proj_kernel(*refs, n_chunks, n_rope, q_scale, dil, tl):
    in_refs, o_ref = refs[:-1], refs[-1]
    x_ref, g_ref, w_ref = in_refs[:3]
    p_ref = in_refs[3] if dil > 1 else None
    tabs = in_refs[-3:] if n_rope else None
    n_sub = tl // A_PERM
    n = A_PERM // dil

    h = _rmsnorm(x_ref[...], g_ref[...]).astype(BF16)
    if dil > 1:
        p = p_ref[...]
        h = jnp.concatenate(
            [_dot(p, h[s * A_PERM:(s + 1) * A_PERM, :]).astype(BF16) for s in range(n_sub)], axis=0)

    def store(val, lo, width):
        val = val.astype(o_ref.dtype)
        if dil == 1:
            o_ref[0, :, lo:lo + width] = val
        else:
            for s in range(n_sub):
                for r in range(dil):
                    a = s * A_PERM + r * n
                    o_ref[r, s * n:(s + 1) * n, lo:lo + width] = val[a:a + n, :]

    for lo in range(0, n_chunks * D_MODEL, PROJ_PIECE):
        acc = _dot(h, w_ref[:, lo:lo + PROJ_PIECE])
        if lo < n_rope * D_MODEL:
            ta, tb, tc = tabs[0][...], tabs[1][...], tabs[2][...]
            for hd in range(PROJ_PIECE // LANES):
                s = acc[:, hd * LANES:(hd + 1) * LANES]
                r = (s * ta + pltpu.roll(s, LANES - ROPE_DIM // 2, 1) * tb
                     + pltpu.roll(s, ROPE_DIM // 2, 1) * tc)
                if lo < D_MODEL and q_scale is not None:
                    r = r * q_scale
                store(r, lo + hd * LANES, LANES)
        else:
            store(acc, lo, PROJ_PIECE)


def _norm_proj(x, g, w, *, dil=1, rope=None, n_rope=0, q_scale=None, tl=512):
    bsz, seq, _ = x.shape
    n = w.shape[1]
    tl = min(tl, seq)
    full = lambda b, i: (0, 0)
    in_specs = [
        pl.BlockSpec((None, tl, D_MODEL), lambda b, i: (b, i, 0)),
        pl.BlockSpec((1, D_MODEL), full),
        pl.BlockSpec((D_MODEL, n), full),
    ]
    args = [x, g.reshape(1, D_MODEL), w]
    if dil > 1:
        in_specs.append(pl.BlockSpec((A_PERM, A_PERM), full))
        args.append(jnp.asarray(_deinterleave_matrix(dil), BF16))
    if n_rope:
        for t in rope:
            in_specs.append(pl.BlockSpec((tl, LANES), lambda b, i: (i, 0)))
            args.append(t)
    return pl.pallas_call(
        functools.partial(_norm_proj_kernel, n_chunks=n // D_MODEL, n_rope=n_rope, q_scale=q_scale,
                          dil=dil, tl=tl),
        grid=(bsz, seq // tl),
        in_specs=in_specs,
        out_specs=pl.BlockSpec((None, dil, tl // dil, n), lambda b, i: (b, 0, i, 0)),
        out_shape=jax.ShapeDtypeStruct((bsz, dil, seq // dil, n), BF16),
        compiler_params=_params(("arbitrary", "arbitrary")),
        name="norm_proj",
    )(*args)


def _attn_kernel(q_ref, k_ref, v_ref, kp_ref, kn_ref, vp_ref, vn_ref, o_ref, lse_ref, kext, vext, *, tq, rows):
    step = pl.program_id(1)
    l0 = (step % (rows // tq)) * tq
    first = (pl.program_id(0) == 0) & (step == 0)

    @pl.when(first)
    def _():
        vext[...] = jnp.ones_like(vext)

    kext[0:A_HALF, :] = kp_ref[...]
    kext[A_HALF:A_HALF + tq, :] = k_ref[...]
    kext[A_HALF + tq:, :] = kn_ref[...]
    for hd in range(A_HEADS):
        src = slice(hd * A_HEAD_DIM, (hd + 1) * A_HEAD_DIM)
        dst = slice(2 * hd * A_HEAD_DIM, (2 * hd + 1) * A_HEAD_DIM)
        vext[0:A_HALF, dst] = vp_ref[:, src]
        vext[A_HALF:A_HALF + tq, dst] = v_ref[:, src]
        vext[A_HALF + tq:, dst] = vn_ref[:, src]

    qi = lax.broadcasted_iota(jnp.int32, (A_QBLK, A_KBLK), 0)
    ci = lax.broadcasted_iota(jnp.int32, (A_QBLK, A_KBLK), 1)
    lane = lax.broadcasted_iota(jnp.int32, (A_QBLK, LANES), 1)
    rel = ci - qi
    for j in range(tq // A_QBLK):
        key = l0 + (j * A_QBLK - A_HALF) + ci
        bias = jnp.where(rel >= 0, jnp.where(rel <= 2 * A_HALF, 0.0, NEG), NEG)
        bias = jnp.where(key >= 0, jnp.where(key < rows, bias, NEG), NEG)
        r0 = j * A_QBLK
        lse_tile = jnp.zeros((A_QBLK, LANES), F32)
        for hd in range(A_HEADS):
            c0 = hd * A_HEAD_DIM
            q = q_ref[r0:r0 + A_QBLK, c0:c0 + A_HEAD_DIM]
            k = kext[r0:r0 + A_KBLK, c0:c0 + A_HEAD_DIM]
            v1 = vext[r0:r0 + A_KBLK, 2 * c0:2 * c0 + 2 * A_HEAD_DIM]
            s = _dot_nt(q, k) + bias
            m = jnp.max(s, axis=1, keepdims=True)
            p = jnp.exp2((s - m).astype(BF16))
            acc = _dot(p, v1)
            den = acc[:, A_HEAD_DIM:]
            o_ref[r0:r0 + A_QBLK, c0:c0 + A_HEAD_DIM] = (acc[:, :A_HEAD_DIM] / den).astype(o_ref.dtype)
            lse_tile = jnp.where(lane == hd, m * LN2 + jnp.log(den), lse_tile)
        lse_ref[r0:r0 + A_QBLK, :] = lse_tile


def _attn_group(qkv, *, tq=512):
    bsz, dil, rows, _ = qkv.shape
    tq = min(tq, rows)
    nt = rows // tq
    nh = tq // A_HALF
    last = rows // A_HALF - 1

    def main(col, width=A_WIDTH):
        return pl.BlockSpec((None, None, tq, width), lambda b, n: (b, n // nt, n % nt, col))

    def halo_prev(col):
        return pl.BlockSpec((None, None, A_HALF, A_WIDTH),
                            lambda b, n: (b, n // nt, jnp.maximum((n % nt) * nh - 1, 0), col))

    def halo_next(col):
        return pl.BlockSpec((None, None, A_HALF, A_WIDTH),
                            lambda b, n: (b, n // nt, jnp.minimum((n % nt + 1) * nh, last), col))

    return pl.pallas_call(
        functools.partial(_attn_kernel, tq=tq, rows=rows),
        grid=(bsz, dil * nt),
        in_specs=[main(0), main(1), main(2), halo_prev(1), halo_next(1), halo_prev(2), halo_next(2)],
        out_specs=[main(0), main(0, LANES)],
        out_shape=[jax.ShapeDtypeStruct((bsz, dil, rows, A_WIDTH), BF16),
                   jax.ShapeDtypeStruct((bsz, dil, rows, LANES), F32)],
        scratch_shapes=[pltpu.VMEM((tq + 2 * A_HALF, A_WIDTH), BF16),
                        pltpu.VMEM((tq + 2 * A_HALF, 2 * A_WIDTH), BF16)],
        compiler_params=_params(("arbitrary", "arbitrary")),
        name="attn_group",
    )(qkv, qkv, qkv, qkv, qkv, qkv, qkv)


def _a_out_kernel(*refs, final, dils, tm):
    n_g = len(dils)
    x_ref = refs[0]
    o_refs = refs[1:1 + n_g]
    l_refs = refs[1 + n_g:1 + 2 * n_g]
    n_perm = sum(1 for d in dils if d > 1)
    p_refs = refs[1 + 2 * n_g:1 + 2 * n_g + n_perm]
    g_ref, wg_ref, wo_ref = refs[1 + 2 * n_g + n_perm:4 + 2 * n_g + n_perm]
    fg_ref = refs[-2] if final else None
    y_ref = refs[-1]

    for sub in range(tm // A_PERM):
        rows = slice(sub * A_PERM, (sub + 1) * A_PERM)

        def block(ref, dil):
            n = A_PERM // dil
            return jnp.concatenate([ref[r, sub * n:(sub + 1) * n, :] for r in range(dil)], axis=0)

        outs, lses = [], []
        pi = 0
        for gi, dil in enumerate(dils):
            o = block(o_refs[gi], dil)
            lse = block(l_refs[gi], dil)
            if dil > 1:
                pt = p_refs[pi][...]
                pi += 1
                o = _dot(pt, o)
                hi, lo = _split2(lse)
                lse = _dot(pt, hi) + _dot(pt, lo)
            else:
                o = o.astype(F32)
            outs.append(o)
            lses.append(lse)

        mx = functools.reduce(jnp.maximum, lses)
        ws = [jnp.exp(l - mx) for l in lses]
        inv = 1.0 / functools.reduce(lambda a, b: a + b, ws)
        ws = [w * inv for w in ws]

        x = x_ref[rows, :]
        h = _rmsnorm(x, g_ref[...]).astype(BF16)
        gate = _silu(_dot(h, wg_ref[...]))
        cols = []
        for hd in range(A_HEADS):
            c0 = hd * A_HEAD_DIM
            o = outs[0][:, c0:c0 + A_HEAD_DIM] * ws[0][:, hd:hd + 1]
            for gi in range(1, n_g):
                o = o + outs[gi][:, c0:c0 + A_HEAD_DIM] * ws[gi][:, hd:hd + 1]
            cols.append((o * gate[:, c0:c0 + A_HEAD_DIM]).astype(BF16))
        y = x + _dot(jnp.concatenate(cols, axis=1), wo_ref[...])
        if final:
            y = _rmsnorm(y, fg_ref[...])
        y_ref[rows, :] = y


def _a_out(x, outs, lses, g, w_gate, w_out, final_g, *, tm=512):
    bsz, seq, _ = x.shape
    tm = min(tm, seq)
    final = final_g is not None
    dils = tuple(o.shape[1] for o in outs)
    tile = pl.BlockSpec((None, tm, D_MODEL), lambda b, i: (b, i, 0))
    vec = pl.BlockSpec((1, D_MODEL), lambda b, i: (0, 0))
    mat = pl.BlockSpec((D_MODEL, D_MODEL), lambda b, i: (0, 0))
    in_specs = [tile]
    args = [x]
    for width, arrs in ((A_WIDTH, outs), (LANES, lses)):
        for d, a in zip(dils, arrs):
            in_specs.append(pl.BlockSpec((None, d, tm // d, width), lambda b, i: (b, 0, i, 0)))
            args.append(a)
    for d in dils:
        if d > 1:
            in_specs.append(pl.BlockSpec((A_PERM, A_PERM), lambda b, i: (0, 0)))
            args.append(jnp.asarray(_deinterleave_matrix(d).T, BF16))
    in_specs += [vec, mat, mat]
    args += [g.reshape(1, D_MODEL), w_gate, w_out]
    if final:
        in_specs.append(vec)
        args.append(final_g.reshape(1, D_MODEL))
    return pl.pallas_call(
        functools.partial(_a_out_kernel, final=final, dils=dils, tm=tm),
        grid=(bsz, seq // tm),
        in_specs=in_specs,
        out_specs=tile,
        out_shape=jax.ShapeDtypeStruct((bsz, seq, D_MODEL), F32),
        compiler_params=_params(("arbitrary", "arbitrary")),
        name="a_out",
    )(*args)


def _rope_tables(seq, dil):
    half = ROPE_DIM // 2
    pos = jnp.arange(seq, dtype=F32)
    inv = ROPE_THETA ** (-(jnp.arange(half, dtype=F32) * 2.0) / ROPE_DIM)
    ang = pos[:, None] * inv[None, :]
    cos, sin = jnp.cos(ang), jnp.sin(ang)
    zeros = jnp.zeros((seq, LANES - ROPE_DIM), F32)
    zh = jnp.zeros((seq, half), F32)
    ta = jnp.concatenate([cos, cos, jnp.ones((seq, LANES - ROPE_DIM), F32)], axis=1)
    tb = jnp.concatenate([-sin, zh, zeros], axis=1)
    tc = jnp.concatenate([zh, sin, zeros], axis=1)

    def order(t):
        if dil == 1:
            return t
        t = t.reshape(seq // A_PERM, A_PERM // dil, dil, LANES)
        return jnp.swapaxes(t, 1, 2).reshape(seq, LANES)

    return order(ta), order(tb), order(tc)


def _mixer_a(x, g, w_in, w_out, final_g):
    seq = x.shape[1]
    n_qkv = 3 * len(A_GROUPS) * A_WIDTH
    outs, lses = [], []
    for grp, (_, dil) in enumerate(A_GROUPS):
        cols = [w_in[:, t * len(A_GROUPS) * A_WIDTH + grp * A_WIDTH:][:, :A_WIDTH] for t in range(3)]
        w_g = jnp.concatenate(cols, axis=1).astype(BF16)
        qkv = _norm_proj(x, g, w_g, dil=dil, rope=_rope_tables(seq, dil), n_rope=2,
                         q_scale=A_HEAD_DIM ** -0.5 / LN2)
        o, lse = _attn_group(qkv)
        outs.append(o)
        lses.append(lse)
    return _a_out(x, outs, lses, g, w_in[:, n_qkv:].astype(BF16), w_out.astype(BF16), final_g)


def _rglru_kernel(*refs, tile, n_tiles, reverse, n_next):
    if reverse:
        (xm_ref, xp_ref, xn_ref, cw_ref, cb_ref, wg_ref, br_ref, bi_ref, lam_ref,
         hf_ref, gate_ref, x_ref, wo_ref, gnext_ref, wnext_ref, out_ref, pnext_ref,
         a_scr, u_scr, h_scr) = refs
    else:
        (xm_ref, xp_ref, xn_ref, cw_ref, cb_ref, wg_ref, br_ref, bi_ref, lam_ref,
         out_ref, a_scr, u_scr, h_scr) = refs
    step = pl.program_id(1)
    t = (n_tiles - 1 - step) if reverse else step

    prev = jnp.where(t > 0, xp_ref[...].astype(F32), 0.0)
    nxt = jnp.where(t < n_tiles - 1, xn_ref[...].astype(F32), 0.0)
    xm = xm_ref[...].astype(F32)
    ext = jnp.concatenate([prev, xm, nxt], axis=0)
    n_ext = tile + 2 * BF16_ROWS
    xm1 = pltpu.roll(ext, 1, 0)[BF16_ROWS:BF16_ROWS + tile, :]
    xm2 = pltpu.roll(ext, 2, 0)[BF16_ROWS:BF16_ROWS + tile, :]
    xp1 = pltpu.roll(ext, n_ext - 1, 0)[BF16_ROWS:BF16_ROWS + tile, :]
    cw = cw_ref[...]
    xc = xm2 * cw[0:1, :] + xm1 * cw[1:2, :] + xm * cw[2:3, :] + xp1 * cw[3:4, :] + cb_ref[...]

    lam = lam_ref[...]
    softplus = jnp.log1p(jnp.exp(-jnp.abs(lam))) + jnp.maximum(-lam, 0.0)
    for n in range(B_BLOCKS):
        c0 = n * B_BLOCK_DIM
        xcn = xc[:, c0:c0 + B_BLOCK_DIM]
        gates = _dot(xcn.astype(BF16), wg_ref[n])
        r = _sigmoid(gates[:, :B_BLOCK_DIM] + br_ref[:, c0:c0 + B_BLOCK_DIM])
        ig = _sigmoid(gates[:, B_BLOCK_DIM:] + bi_ref[:, c0:c0 + B_BLOCK_DIM])
        log_a = -B_C * r * softplus[:, c0:c0 + B_BLOCK_DIM]
        a = jnp.exp(log_a)
        mult = jnp.sqrt(-jnp.tanh(log_a) * (a * a + 1.0))
        a_scr[:, c0:c0 + B_BLOCK_DIM] = a
        u_scr[:, c0:c0 + B_BLOCK_DIM] = mult * (ig * xcn)

    @pl.when(step == 0)
    def _():
        h_scr[...] = jnp.zeros_like(h_scr)

    n_groups = tile // SUBLANES
    row8 = lax.broadcasted_iota(jnp.int32, (SUBLANES, D_MODEL), 0)

    def body(gi, h):
        g = (n_groups - 1 - gi) if reverse else gi
        off = pl.multiple_of(g * SUBLANES, SUBLANES)
        a8 = a_scr[pl.ds(off, SUBLANES), :]
        u8 = u_scr[pl.ds(off, SUBLANES), :]
        for s in (1, 2, 4):
            if reverse:
                a_sh = pltpu.roll(a8, SUBLANES - s, 0)
                u_sh = pltpu.roll(u8, SUBLANES - s, 0)
                keep = row8 < SUBLANES - s
            else:
                a_sh = pltpu.roll(a8, s, 0)
                u_sh = pltpu.roll(u8, s, 0)
                keep = row8 >= s
            u8 = jnp.where(keep, a8 * u_sh + u8, u8)
            a8 = jnp.where(keep, a8 * a_sh, a8)
        hs = a8 * h + u8
        u_scr[pl.ds(off, SUBLANES), :] = hs
        return hs[0:1, :] if reverse else hs[SUBLANES - 1:SUBLANES, :]

    h_scr[...] = lax.fori_loop(0, n_groups, body, h_scr[...], unroll=4)

    if reverse:
        y = hf_ref[...] + u_scr[...]
        z = (y * _silu(gate_ref[...].astype(F32))).astype(BF16)
        x_new = x_ref[...] + _dot(z, wo_ref[...])
        out_ref[...] = x_new
        h_next = _rmsnorm(x_new, gnext_ref[...]).astype(BF16)
        for c in range(n_next):
            cols = slice(c * D_MODEL, (c + 1) * D_MODEL)
            pnext_ref[:, cols] = _dot(h_next, wnext_ref[:, cols]).astype(pnext_ref.dtype)
    else:
        out_ref[...] = u_scr[...]


def _rglru(proj, conv_w, conv_b, wg, b_r, b_i, lam, *, reverse, hf=None, x=None, w_out=None,
           g_next=None, w_next=None, tile=512):
    bsz, seq, _ = proj.shape
    tile = min(tile, seq)
    n_tiles = seq // tile
    per = tile // BF16_ROWS
    last = seq // BF16_ROWS - 1

    def tix(i):
        return (n_tiles - 1 - i) if reverse else i

    main = pl.BlockSpec((None, tile, D_MODEL), lambda b, i: (b, tix(i), 0))
    halo_p = pl.BlockSpec((None, BF16_ROWS, D_MODEL), lambda b, i: (b, jnp.maximum(tix(i) * per - 1, 0), 0))
    halo_n = pl.BlockSpec((None, BF16_ROWS, D_MODEL), lambda b, i: (b, jnp.minimum((tix(i) + 1) * per, last), 0))
    vec = pl.BlockSpec((1, D_MODEL), lambda b, i: (0, 0))
    in_specs = [main, halo_p, halo_n,
                pl.BlockSpec((4, D_MODEL), lambda b, i: (0, 0)), vec,
                pl.BlockSpec((B_BLOCKS, B_BLOCK_DIM, 2 * B_BLOCK_DIM), lambda b, i: (0, 0, 0)),
                vec, vec, vec]
    args = [proj, proj, proj, conv_w, conv_b.reshape(1, D_MODEL), wg,
            b_r.reshape(1, D_MODEL), b_i.reshape(1, D_MODEL), lam.reshape(1, D_MODEL)]
    out_specs = main
    out_shape = jax.ShapeDtypeStruct((bsz, seq, D_MODEL), F32)
    n_next = 0
    if reverse:
        n_next = w_next.shape[1] // D_MODEL
        once = pl.Buffered(1)
        in_specs += [main, pl.BlockSpec((None, tile, D_MODEL), lambda b, i: (b, tix(i), 1)), main,
                     pl.BlockSpec((D_MODEL, D_MODEL), lambda b, i: (0, 0), pipeline_mode=once), vec,
                     pl.BlockSpec(w_next.shape, lambda b, i: (0, 0), pipeline_mode=once)]
        args += [hf, proj, x, w_out, g_next.reshape(1, D_MODEL), w_next]
        out_specs = [main, pl.BlockSpec((None, tile, w_next.shape[1]), lambda b, i: (b, tix(i), 0))]
        out_shape = [out_shape, jax.ShapeDtypeStruct((bsz, seq, w_next.shape[1]), BF16)]
    return pl.pallas_call(
        functools.partial(_rglru_kernel, tile=tile, n_tiles=n_tiles, reverse=reverse, n_next=n_next),
        grid=(bsz, n_tiles),
        in_specs=in_specs,
        out_specs=out_specs,
        out_shape=out_shape,
        scratch_shapes=[pltpu.VMEM((tile, D_MODEL), F32), pltpu.VMEM((tile, D_MODEL), F32),
                        pltpu.VMEM((1, D_MODEL), F32)],
        compiler_params=_params(("arbitrary", "arbitrary")),
        name="rglru_bwd_out" if reverse else "rglru_fwd",
    )(*args)


def _mixer_b(x, g, w_in, conv_w, conv_b, w_r, b_r, w_i, b_i, lam, w_out, g_next, w_next):
    bsz, seq, _ = x.shape
    proj = _norm_proj(x, g, w_in.astype(BF16)).reshape(bsz, seq, 2 * D_MODEL)
    wg = [jnp.concatenate([w_r[d], w_i[d]], axis=-1).astype(BF16) for d in range(2)]
    hf = _rglru(proj, conv_w, conv_b, wg[0], b_r[0], b_i[0], lam[0], reverse=False)
    return _rglru(proj, conv_w, conv_b, wg[1], b_r[1], b_i[1], lam[1], reverse=True,
                  hf=hf, x=x, w_out=w_out.astype(BF16), g_next=g_next, w_next=w_next.astype(BF16))


def _split3(x):
    hi = x.astype(BF16)
    r1 = x - hi.astype(F32)
    mid = r1.astype(BF16)
    lo = (r1 - mid.astype(F32)).astype(BF16)
    return hi, mid, lo


def _hgrn_kernel(*refs, tile, reverse):
    if reverse:
        (q_ref, z_ref, v_ref, lb_ref, tri_ref, of_ref, gate_ref, x_ref, gn_ref, wo_ref,
         out_ref, st_scr, o_scr) = refs
    else:
        q_ref, z_ref, v_ref, lb_ref, tri_ref, out_ref, st_scr, o_scr = refs

    @pl.when(pl.program_id(1) == 0)
    def _():
        st_scr[...] = jnp.zeros_like(st_scr)

    n_blk = C_TILE // C_CHUNK
    order = list(reversed(range(n_blk))) if reverse else list(range(n_blk))
    mid_row = C_CHUNK // 2 - 1 if reverse else C_CHUNK // 2
    end_row = 0 if reverse else C_CHUNK - 1
    lb = lb_ref[...]
    one_m_lb = 1.0 - lb
    tri = tri_ref[...]

    n_sub = tile // C_TILE
    for sub in (reversed(range(n_sub)) if reverse else range(n_sub)):
        base = sub * C_TILE
        qt, kt, rho, blk_total = {}, {}, {}, {}
        for j in order:
            r0 = base + j * C_CHUNK
            z = z_ref[r0:r0 + C_CHUNK, :].astype(F32)
            half_t = 0.5 * jnp.tanh(0.5 * z)
            sig_pos = 0.5 + half_t
            sig_neg = 0.5 - half_t
            logf = jnp.log(lb + one_m_lb * sig_pos)
            b = _dot(tri, jnp.concatenate(_split3(logf), axis=0))
            b_mid = b[mid_row:mid_row + 1, :]
            qt[j] = q_ref[r0:r0 + C_CHUNK, :].astype(F32) * jnp.exp(b - b_mid)
            kt[j] = (one_m_lb * sig_neg) * jnp.exp(b_mid - b)
            rho[j] = b_mid
            blk_total[j] = b[end_row:end_row + 1, :]

        off = jnp.zeros_like(lb)
        for j in order:
            rho[j] = rho[j] + off
            off = off + blk_total[j]
        b_end = off
        decay = jnp.exp(b_end)

        qd = jnp.concatenate([(qt[j] * jnp.exp(rho[j])).astype(BF16) for j in range(n_blk)], axis=0)
        kd = jnp.concatenate([(kt[j] * jnp.exp(b_end - rho[j])).astype(BF16) for j in range(n_blk)], axis=0)

        zeros = jnp.zeros((C_CHUNK, D_MODEL), BF16)
        keys = {}
        for i in range(n_blk):
            parts = []
            for j in range(n_blk):
                if j == i:
                    parts.append(kt[j].astype(BF16))
                elif order.index(j) < order.index(i):
                    parts.append((kt[j] * jnp.exp(rho[i] - rho[j])).astype(BF16))
                else:
                    parts.append(zeros)
            keys[i] = jnp.concatenate(parts, axis=0)
        qt = {j: qt[j].astype(BF16) for j in qt}

        ti = lax.broadcasted_iota(jnp.int32, (C_TILE, C_TILE), 0)
        si = lax.broadcasted_iota(jnp.int32, (C_TILE, C_TILE), 1)
        visible = (si >= ti) if reverse else (si <= ti)
        zq = jnp.zeros((C_CHUNK, C_KDIM), BF16)
        rows = slice(base, base + C_TILE)
        for hd in range(C_HEADS):
            cs = slice(hd * C_KDIM, (hd + 1) * C_KDIM)
            q_diag = jnp.concatenate(
                [jnp.concatenate([qt[i][:, cs] if j == i else zq for j in range(n_blk)], axis=1)
                 for i in range(n_blk)], axis=0)
            k_cat = jnp.concatenate([keys[i][:, cs] for i in range(n_blk)], axis=1)
            att = jnp.where(visible, _dot_nt(q_diag, k_cat), 0.0).astype(BF16)
            st = st_scr[hd]
            o_scr[rows, cs] = _dot(att, v_ref[rows, cs]) + _dot_nt(qd[:, cs], st.astype(BF16))
            st_scr[hd] = decay[:, cs] * st + _dot_tn(v_ref[rows, cs], kd[:, cs])

    if reverse:
        gn = gn_ref[...]
        for hd in range(C_HEADS):
            c0 = hd * C_KDIM
            o = of_ref[:, c0:c0 + C_KDIM] + o_scr[:, c0:c0 + C_KDIM]
            o = o * lax.rsqrt(jnp.mean(o * o, axis=-1, keepdims=True) + EPS) * gn
            o_scr[:, c0:c0 + C_KDIM] = o
        z = (o_scr[...] * _silu(gate_ref[...].astype(F32))).astype(BF16)
        out_ref[...] = x_ref[...] + _dot(z, wo_ref[...])
    else:
        out_ref[...] = o_scr[...]


def _hgrn(proj, lb, *, reverse, of=None, x=None, gn=None, w_out=None, tile=512):
    bsz, seq, _ = proj.shape
    tile = min(tile, seq)
    n_tiles = seq // tile

    def tix(i):
        return (n_tiles - 1 - i) if reverse else i

    def col(j):
        return pl.BlockSpec((None, tile, D_MODEL), lambda b, i: (b, tix(i), j))

    ti = lax.broadcasted_iota(jnp.int32, (C_CHUNK, C_CHUNK), 0)
    si = lax.broadcasted_iota(jnp.int32, (C_CHUNK, C_CHUNK), 1)
    tri = ((si >= ti) if reverse else (si <= ti)).astype(BF16)
    tri = jnp.concatenate([tri, tri, tri], axis=1)
    in_specs = [col(0), col(2 if reverse else 1), col(3),
                pl.BlockSpec((1, D_MODEL), lambda b, i: (0, 0)),
                pl.BlockSpec((C_CHUNK, 3 * C_CHUNK), lambda b, i: (0, 0))]
    args = [proj, proj, proj, lb.reshape(1, D_MODEL), tri]
    if reverse:
        in_specs += [col(0), col(4), col(0),
                     pl.BlockSpec((1, C_KDIM), lambda b, i: (0, 0)),
                     pl.BlockSpec((D_MODEL, D_MODEL), lambda b, i: (0, 0))]
        args += [of, proj, x, gn.reshape(1, C_KDIM), w_out]
    return pl.pallas_call(
        functools.partial(_hgrn_kernel, tile=tile, reverse=reverse),
        grid=(bsz, n_tiles),
        in_specs=in_specs,
        out_specs=col(0),
        out_shape=jax.ShapeDtypeStruct((bsz, seq, D_MODEL), F32),
        scratch_shapes=[pltpu.VMEM((C_HEADS, C_KDIM, C_KDIM), F32), pltpu.VMEM((tile, D_MODEL), F32)],
        compiler_params=_params(("arbitrary", "arbitrary")),
        name="hgrn_bwd_out" if reverse else "hgrn_fwd",
    )(*args)


def _mixer_c(x, proj, lb, gn, w_out):
    of = _hgrn(proj, lb[0], reverse=False)
    return _hgrn(proj, lb[1], reverse=True, of=of, x=x, gn=gn, w_out=w_out.astype(BF16))


def _trunk(x, norm_g, final_g, a_w_in, a_w_out, b_w_in, b_conv_w, b_conv_b, b_w_r, b_b_r, b_w_i, b_b_i,
           b_lambda, b_w_out, c_w_in, c_lower_bounds, c_gnorm_g, c_w_out):
    depth = norm_g.shape[0]
    sm = jax.nn.softmax(c_lower_bounds.astype(F32), axis=0)
    lbs = jnp.cumsum(sm, axis=0) - sm[0]
    if depth % N_MIXERS != 1 or depth < 1:
        raise NotImplementedError("trunk depth must end on an attention layer")
    proj_c = None
    for layer in range(depth):
        kind = layer % N_MIXERS
        j = layer // N_MIXERS
        if kind == 0:
            x = _mixer_a(x, norm_g[layer], a_w_in[j], a_w_out[j], final_g if layer == depth - 1 else None)
        elif kind == 1:
            x, proj_c = _mixer_b(x, norm_g[layer], b_w_in[j], b_conv_w[j], b_conv_b[j], b_w_r[j], b_b_r[j],
                                 b_w_i[j], b_b_i[j], b_lambda[j], b_w_out[j], norm_g[layer + 1], c_w_in[j])
        else:
            x = _mixer_c(x, proj_c, lbs[layer], c_gnorm_g[j], c_w_out[j])
    return x


def kernel(x_prompt, x_sample, norm_g, final_g, a_w_in, a_w_out, b_w_in, b_conv_w, b_conv_b, b_w_r, b_b_r,
           b_w_i, b_b_i, b_lambda, b_w_out, c_w_in, c_lower_bounds, c_gnorm_g, c_w_out):
    weights = (norm_g, final_g, a_w_in, a_w_out, b_w_in, b_conv_w, b_conv_b, b_w_r, b_b_r, b_w_i, b_b_i,
               b_lambda, b_w_out, c_w_in, c_lower_bounds, c_gnorm_g, c_w_out)
    return (_trunk(x_prompt, *weights), _trunk(x_sample, *weights))
```

```python
import functools

import numpy as np
import jax
import jax.numpy as jnp
from jax import lax
from jax.experimental import pallas as pl
from jax.experimental.pallas import tpu as pltpu

F32 = jnp.float32
BF16 = jnp.bfloat16

D_MODEL = 1024
EPS = 1e-6
LANES = 128
SUBLANES = 8
BF16_ROWS = 16
VMEM_LIMIT = 56 * 1024 * 1024

N_MIXERS = 3
A_GROUPS = ((128, 1), (512, 4), (2048, 16))
A_HEADS = 8
A_HEAD_DIM = 128
A_WIDTH = A_HEADS * A_HEAD_DIM
A_HALF = 64
ROPE_THETA = 500000.0
ROPE_DIM = A_HEAD_DIM // 4
A_QBLK = 128
A_KBLK = A_QBLK + 2 * A_HALF
A_PERM = 256
PROJ_PIECE = 256
NEG = -1e30
LN2 = 0.6931471805599453
B_BLOCKS = 8
B_BLOCK_DIM = D_MODEL // B_BLOCKS
B_C = 8.0
C_HEADS = 8
C_KDIM = 128
C_CHUNK = 64
C_TILE = 256


def _params(semantics):
    return pltpu.CompilerParams(dimension_semantics=semantics, vmem_limit_bytes=VMEM_LIMIT)


def _rmsnorm(x, g):
    ms = jnp.mean(x * x, axis=-1, keepdims=True)
    return x * lax.rsqrt(ms + EPS) * g


def _sigmoid(x):
    return 0.5 * jnp.tanh(0.5 * x) + 0.5


def _silu(g):
    return g * _sigmoid(g)


def _dot(a, b):
    return jnp.dot(a, b, preferred_element_type=F32)


def _dot_nt(a, b):
    return lax.dot_general(a, b, (((1,), (1,)), ((), ())), preferred_element_type=F32)


def _dot_tn(a, b):
    return lax.dot_general(a, b, (((0,), (0,)), ((), ())), preferred_element_type=F32)


def _split2(x):
    hi = x.astype(BF16)
    return hi, (x - hi.astype(F32)).astype(BF16)


def _deinterleave_matrix(dil):
    n = A_PERM // dil
    idx = np.arange(A_PERM)
    p = np.zeros((A_PERM, A_PERM), np.float32)
    p[idx, (idx % n) * dil + idx // n] = 1.0
    return p


def _norm_proj_kernel(*refs, n_chunks, n_rope, q_scale, dil, tl):
    in_refs, o_ref = refs[:-1], refs[-1]
    x_ref, g_ref, w_ref = in_refs[:3]
    p_ref = in_refs[3] if dil > 1 else None
    if n_rope:
        half = ROPE_DIM // 2
        t = in_refs[-1][...]
        lane = lax.broadcasted_iota(jnp.int32, t.shape, 1)
        lo_half, hi_half = lane < half, lane < ROPE_DIM
        ta = jnp.where(lo_half, t, jnp.where(hi_half, pltpu.roll(t, half, 1), 1.0))
        tb = jnp.where(lo_half, -pltpu.roll(t, LANES - half, 1), 0.0)
        tc = jnp.where(lo_half, 0.0, jnp.where(hi_half, t, 0.0))
    n_sub = tl // A_PERM
    n = A_PERM // dil

    h = _rmsnorm(x_ref[...], g_ref[...]).astype(BF16)
    if dil > 1:
        p = p_ref[...]
        h = jnp.concatenate(
            [_dot(p, h[s * A_PERM:(s + 1) * A_PERM, :]).astype(BF16) for s in range(n_sub)], axis=0)

    def store(val, lo, width):
        val = val.astype(o_ref.dtype)
        if dil == 1:
            o_ref[0, :, lo:lo + width] = val
        else:
            for s in range(n_sub):
                for r in range(dil):
                    a = s * A_PERM + r * n
                    o_ref[r, s * n:(s + 1) * n, lo:lo + width] = val[a:a + n, :]

    for lo in range(0, n_chunks * D_MODEL, PROJ_PIECE):
        acc = _dot(h, w_ref[:, lo:lo + PROJ_PIECE])
        if lo < n_rope * D_MODEL:
            for hd in range(PROJ_PIECE // LANES):
                s = acc[:, hd * LANES:(hd + 1) * LANES]
                r = (s * ta + pltpu.roll(s, LANES - ROPE_DIM // 2, 1) * tb
                     + pltpu.roll(s, ROPE_DIM // 2, 1) * tc)
                if lo < D_MODEL and q_scale is not None:
                    r = r * q_scale
                store(r, lo + hd * LANES, LANES)
        else:
            store(acc, lo, PROJ_PIECE)


def _norm_proj(x, g, w, *, dil=1, rope=None, n_rope=0, q_scale=None, tl=1024):
    bsz, seq, _ = x.shape
    n = w.shape[1]
    tl = min(tl, seq)
    full = lambda b, i: (0, 0)
    in_specs = [
        pl.BlockSpec((None, tl, D_MODEL), lambda b, i: (b, i, 0)),
        pl.BlockSpec((1, D_MODEL), full),
        pl.BlockSpec((D_MODEL, n), full),
    ]
    args = [x, g.reshape(1, D_MODEL), w]
    if dil > 1:
        in_specs.append(pl.BlockSpec((A_PERM, A_PERM), full))
        args.append(jnp.asarray(_deinterleave_matrix(dil), BF16))
    if n_rope:
        in_specs.append(pl.BlockSpec((tl, LANES), lambda b, i: (i, 0)))
        args.append(rope)
    return pl.pallas_call(
        functools.partial(_norm_proj_kernel, n_chunks=n // D_MODEL, n_rope=n_rope, q_scale=q_scale,
                          dil=dil, tl=tl),
        grid=(bsz, seq // tl),
        in_specs=in_specs,
        out_specs=pl.BlockSpec((None, dil, tl // dil, n), lambda b, i: (b, 0, i, 0)),
        out_shape=jax.ShapeDtypeStruct((bsz, dil, seq // dil, n), BF16),
        compiler_params=_params(("arbitrary", "arbitrary")),
        name="norm_proj",
    )(*args)


def _attn_kernel(q_ref, k_ref, v_ref, kp_ref, kn_ref, vp_ref, vn_ref, o_ref, lse_ref, kext, vext, *, tq, rows):
    step = pl.program_id(1)
    l0 = (step % (rows // tq)) * tq
    first = (pl.program_id(0) == 0) & (step == 0)

    @pl.when(first)
    def _():
        vext[...] = jnp.ones_like(vext)

    kext[0:A_HALF, :] = kp_ref[...]
    kext[A_HALF:A_HALF + tq, :] = k_ref[...]
    kext[A_HALF + tq:, :] = kn_ref[...]
    for hd in range(A_HEADS):
        src = slice(hd * A_HEAD_DIM, (hd + 1) * A_HEAD_DIM)
        dst = slice(2 * hd * A_HEAD_DIM, (2 * hd + 1) * A_HEAD_DIM)
        vext[0:A_HALF, dst] = vp_ref[:, src]
        vext[A_HALF:A_HALF + tq, dst] = v_ref[:, src]
        vext[A_HALF + tq:, dst] = vn_ref[:, src]

    qi = lax.broadcasted_iota(jnp.int32, (A_QBLK, A_KBLK), 0)
    ci = lax.broadcasted_iota(jnp.int32, (A_QBLK, A_KBLK), 1)
    lane = lax.broadcasted_iota(jnp.int32, (A_QBLK, LANES), 1)
    rel = ci - qi
    for j in range(tq // A_QBLK):
        key = l0 + (j * A_QBLK - A_HALF) + ci
        bias = jnp.where(rel >= 0, jnp.where(rel <= 2 * A_HALF, 0.0, NEG), NEG)
        bias = jnp.where(key >= 0, jnp.where(key < rows, bias, NEG), NEG)
        r0 = j * A_QBLK
        lse_tile = jnp.zeros((A_QBLK, LANES), F32)
        for hd in range(A_HEADS):
            c0 = hd * A_HEAD_DIM
            q = q_ref[r0:r0 + A_QBLK, c0:c0 + A_HEAD_DIM]
            k = kext[r0:r0 + A_KBLK, c0:c0 + A_HEAD_DIM]
            v1 = vext[r0:r0 + A_KBLK, 2 * c0:2 * c0 + 2 * A_HEAD_DIM]
            s = _dot_nt(q, k) + bias
            m = jnp.max(s, axis=1, keepdims=True)
            p = jnp.exp2((s - m).astype(BF16))
            acc = _dot(p, v1)
            den = acc[:, A_HEAD_DIM:]
            o_ref[r0:r0 + A_QBLK, c0:c0 + A_HEAD_DIM] = (acc[:, :A_HEAD_DIM] / den).astype(o_ref.dtype)
            lse_tile = jnp.where(lane == hd, m * LN2 + jnp.log(den), lse_tile)
        lse_ref[r0:r0 + A_QBLK, :] = lse_tile


def _attn_group(qkv, *, tq=1024):
    bsz, dil, rows, _ = qkv.shape
    tq = min(tq, rows)
    nt = rows // tq
    nh = tq // A_HALF
    last = rows // A_HALF - 1

    def main(col, width=A_WIDTH):
        return pl.BlockSpec((None, None, tq, width), lambda b, n: (b, n // nt, n % nt, col))

    def halo_prev(col):
        return pl.BlockSpec((None, None, A_HALF, A_WIDTH),
                            lambda b, n: (b, n // nt, jnp.maximum((n % nt) * nh - 1, 0), col))

    def halo_next(col):
        return pl.BlockSpec((None, None, A_HALF, A_WIDTH),
                            lambda b, n: (b, n // nt, jnp.minimum((n % nt + 1) * nh, last), col))

    return pl.pallas_call(
        functools.partial(_attn_kernel, tq=tq, rows=rows),
        grid=(bsz, dil * nt),
        in_specs=[main(0), main(1), main(2), halo_prev(1), halo_next(1), halo_prev(2), halo_next(2)],
        out_specs=[main(0), main(0, LANES)],
        out_shape=[jax.ShapeDtypeStruct((bsz, dil, rows, A_WIDTH), BF16),
                   jax.ShapeDtypeStruct((bsz, dil, rows, LANES), F32)],
        scratch_shapes=[pltpu.VMEM((tq + 2 * A_HALF, A_WIDTH), BF16),
                        pltpu.VMEM((tq + 2 * A_HALF, 2 * A_WIDTH), BF16)],
        compiler_params=_params(("arbitrary", "arbitrary")),
        name="attn_group",
    )(qkv, qkv, qkv, qkv, qkv, qkv, qkv)


def _a_out_kernel(*refs, final, dils, tm):
    n_g = len(dils)
    x_ref = refs[0]
    o_refs = refs[1:1 + n_g]
    l_refs = refs[1 + n_g:1 + 2 * n_g]
    n_perm = sum(1 for d in dils if d > 1)
    p_refs = refs[1 + 2 * n_g:1 + 2 * n_g + n_perm]
    g_ref, wg_ref, wo_ref = refs[1 + 2 * n_g + n_perm:4 + 2 * n_g + n_perm]
    fg_ref = refs[-2] if final else None
    y_ref = refs[-1]

    for sub in range(tm // A_PERM):
        rows = slice(sub * A_PERM, (sub + 1) * A_PERM)

        def block(ref, dil):
            n = A_PERM // dil
            return jnp.concatenate([ref[r, sub * n:(sub + 1) * n, :] for r in range(dil)], axis=0)

        outs, lses = [], []
        pi = 0
        for gi, dil in enumerate(dils):
            o = block(o_refs[gi], dil)
            lse = block(l_refs[gi], dil)
            if dil > 1:
                pt = p_refs[pi][...]
                pi += 1
                o = _dot(pt, o)
                hi, lo = _split2(lse)
                lse = _dot(pt, hi) + _dot(pt, lo)
            else:
                o = o.astype(F32)
            outs.append(o)
            lses.append(lse)

        mx = functools.reduce(jnp.maximum, lses)
        ws = [jnp.exp(l - mx) for l in lses]
        inv = 1.0 / functools.reduce(lambda a, b: a + b, ws)
        ws = [w * inv for w in ws]

        x = x_ref[rows, :]
        h = _rmsnorm(x, g_ref[...]).astype(BF16)
        gate = _silu(_dot(h, wg_ref[...]))
        cols = []
        for hd in range(A_HEADS):
            c0 = hd * A_HEAD_DIM
            o = outs[0][:, c0:c0 + A_HEAD_DIM] * ws[0][:, hd:hd + 1]
            for gi in range(1, n_g):
                o = o + outs[gi][:, c0:c0 + A_HEAD_DIM] * ws[gi][:, hd:hd + 1]
            cols.append((o * gate[:, c0:c0 + A_HEAD_DIM]).astype(BF16))
        y = x + _dot(jnp.concatenate(cols, axis=1), wo_ref[...])
        if final:
            y = _rmsnorm(y, fg_ref[...])
        y_ref[rows, :] = y


def _a_out(x, outs, lses, g, w_gate, w_out, final_g, *, tm=1024):
    bsz, seq, _ = x.shape
    tm = min(tm, seq)
    final = final_g is not None
    dils = tuple(o.shape[1] for o in outs)
    tile = pl.BlockSpec((None, tm, D_MODEL), lambda b, i: (b, i, 0))
    vec = pl.BlockSpec((1, D_MODEL), lambda b, i: (0, 0))
    mat = pl.BlockSpec((D_MODEL, D_MODEL), lambda b, i: (0, 0))
    in_specs = [tile]
    args = [x]
    for width, arrs in ((A_WIDTH, outs), (LANES, lses)):
        for d, a in zip(dils, arrs):
            in_specs.append(pl.BlockSpec((None, d, tm // d, width), lambda b, i: (b, 0, i, 0)))
            args.append(a)
    for d in dils:
        if d > 1:
            in_specs.append(pl.BlockSpec((A_PERM, A_PERM), lambda b, i: (0, 0)))
            args.append(jnp.asarray(_deinterleave_matrix(d).T, BF16))
    in_specs += [vec, mat, mat]
    args += [g.reshape(1, D_MODEL), w_gate, w_out]
    if final:
        in_specs.append(vec)
        args.append(final_g.reshape(1, D_MODEL))
    return pl.pallas_call(
        functools.partial(_a_out_kernel, final=final, dils=dils, tm=tm),
        grid=(bsz, seq // tm),
        in_specs=in_specs,
        out_specs=tile,
        out_shape=jax.ShapeDtypeStruct((bsz, seq, D_MODEL), F32),
        compiler_params=_params(("arbitrary", "arbitrary")),
        name="a_out",
    )(*args)


def _rope_tables(seq, dil):
    half = ROPE_DIM // 2
    pos = jnp.arange(seq, dtype=F32)
    inv = ROPE_THETA ** (-(jnp.arange(half, dtype=F32) * 2.0) / ROPE_DIM)
    ang = pos[:, None] * inv[None, :]
    t = jnp.concatenate([jnp.cos(ang), jnp.sin(ang), jnp.zeros((seq, LANES - ROPE_DIM), F32)], axis=1)
    if dil > 1:
        t = t.reshape(seq // A_PERM, A_PERM // dil, dil, LANES)
        t = jnp.swapaxes(t, 1, 2).reshape(seq, LANES)
    return t


def _mixer_a(x, g, w_in, w_out, final_g):
    seq = x.shape[1]
    n_qkv = 3 * len(A_GROUPS) * A_WIDTH
    outs, lses = [], []
    for grp, (_, dil) in enumerate(A_GROUPS):
        cols = [w_in[:, t * len(A_GROUPS) * A_WIDTH + grp * A_WIDTH:][:, :A_WIDTH] for t in range(3)]
        w_g = jnp.concatenate(cols, axis=1).astype(BF16)
        qkv = _norm_proj(x, g, w_g, dil=dil, rope=_rope_tables(seq, dil), n_rope=2,
                         q_scale=A_HEAD_DIM ** -0.5 / LN2)
        o, lse = _attn_group(qkv)
        outs.append(o)
        lses.append(lse)
    return _a_out(x, outs, lses, g, w_in[:, n_qkv:].astype(BF16), w_out.astype(BF16), final_g)


def _rglru_kernel(*refs, tile, n_tiles, reverse, n_next):
    step = pl.program_id(1)
    if reverse:
        (xc_ref, wg_ref, br_ref, bi_ref, lam_ref,
         hf_ref, gate_ref, x_ref, wo_ref, gnext_ref, wnext_ref, out_ref, pnext_ref,
         a_scr, u_scr, h_scr) = refs
        xc = xc_ref[...]
    else:
        (xm_ref, xp_ref, xn_ref, cw_ref, cb_ref, wg_ref, br_ref, bi_ref, lam_ref,
         out_ref, xc_ref, a_scr, u_scr, h_scr) = refs
        prev = jnp.where(step > 0, xp_ref[...].astype(F32), 0.0)
        nxt = jnp.where(step < n_tiles - 1, xn_ref[...].astype(F32), 0.0)
        xm = xm_ref[...].astype(F32)
        ext = jnp.concatenate([prev, xm, nxt], axis=0)
        n_ext = tile + 2 * BF16_ROWS
        xm1 = pltpu.roll(ext, 1, 0)[BF16_ROWS:BF16_ROWS + tile, :]
        xm2 = pltpu.roll(ext, 2, 0)[BF16_ROWS:BF16_ROWS + tile, :]
        xp1 = pltpu.roll(ext, n_ext - 1, 0)[BF16_ROWS:BF16_ROWS + tile, :]
        cw = cw_ref[...]
        xc = xm2 * cw[0:1, :] + xm1 * cw[1:2, :] + xm * cw[2:3, :] + xp1 * cw[3:4, :] + cb_ref[...]
        xc_ref[...] = xc

    lam = lam_ref[...]
    softplus = jnp.log1p(jnp.exp(-jnp.abs(lam))) + jnp.maximum(-lam, 0.0)
    for n in range(B_BLOCKS):
        c0 = n * B_BLOCK_DIM
        xcn = xc[:, c0:c0 + B_BLOCK_DIM]
        gates = _dot(xcn.astype(BF16), wg_ref[n])
        r = _sigmoid(gates[:, :B_BLOCK_DIM] + br_ref[:, c0:c0 + B_BLOCK_DIM])
        ig = _sigmoid(gates[:, B_BLOCK_DIM:] + bi_ref[:, c0:c0 + B_BLOCK_DIM])
        log_a = -B_C * r * softplus[:, c0:c0 + B_BLOCK_DIM]
        a = jnp.exp(log_a)
        mult = jnp.sqrt(-jnp.tanh(log_a) * (a * a + 1.0))
        a_scr[:, c0:c0 + B_BLOCK_DIM] = a
        u_scr[:, c0:c0 + B_BLOCK_DIM] = mult * (ig * xcn)

    @pl.when(step == 0)
    def _():
        h_scr[...] = jnp.zeros_like(h_scr)

    n_groups = tile // SUBLANES
    row8 = lax.broadcasted_iota(jnp.int32, (SUBLANES, D_MODEL), 0)

    def body(gi, h):
        g = (n_groups - 1 - gi) if reverse else gi
        off = pl.multiple_of(g * SUBLANES, SUBLANES)
        a8 = a_scr[pl.ds(off, SUBLANES), :]
        u8 = u_scr[pl.ds(off, SUBLANES), :]
        for s in (1, 2, 4):
            if reverse:
                a_sh = pltpu.roll(a8, SUBLANES - s, 0)
                u_sh = pltpu.roll(u8, SUBLANES - s, 0)
                keep = row8 < SUBLANES - s
            else:
                a_sh = pltpu.roll(a8, s, 0)
                u_sh = pltpu.roll(u8, s, 0)
                keep = row8 >= s
            u8 = jnp.where(keep, a8 * u_sh + u8, u8)
            a8 = jnp.where(keep, a8 * a_sh, a8)
        hs = a8 * h + u8
        u_scr[pl.ds(off, SUBLANES), :] = hs
        return hs[0:1, :] if reverse else hs[SUBLANES - 1:SUBLANES, :]

    h_scr[...] = lax.fori_loop(0, n_groups, body, h_scr[...], unroll=4)

    if reverse:
        y = hf_ref[...] + u_scr[...]
        z = (y * _silu(gate_ref[...].astype(F32))).astype(BF16)
        x_new = x_ref[...] + _dot(z, wo_ref[...])
        out_ref[...] = x_new
        h_next = _rmsnorm(x_new, gnext_ref[...]).astype(BF16)
        for c in range(n_next):
            cols = slice(c * D_MODEL, (c + 1) * D_MODEL)
            pnext_ref[:, cols] = _dot(h_next, wnext_ref[:, cols]).astype(pnext_ref.dtype)
    else:
        out_ref[...] = u_scr[...]


def _rglru(proj, conv_w, conv_b, wg, b_r, b_i, lam, *, reverse, xc=None, hf=None, x=None, w_out=None,
           g_next=None, w_next=None, tile=512):
    bsz, seq, _ = proj.shape
    tile = min(tile, seq)
    n_tiles = seq // tile
    per = tile // BF16_ROWS
    last = seq // BF16_ROWS - 1

    def tix(i):
        return (n_tiles - 1 - i) if reverse else i

    main = pl.BlockSpec((None, tile, D_MODEL), lambda b, i: (b, tix(i), 0))
    vec = pl.BlockSpec((1, D_MODEL), lambda b, i: (0, 0))
    gate_w = pl.BlockSpec((B_BLOCKS, B_BLOCK_DIM, 2 * B_BLOCK_DIM), lambda b, i: (0, 0, 0))
    gate_args = [wg, b_r.reshape(1, D_MODEL), b_i.reshape(1, D_MODEL), lam.reshape(1, D_MODEL)]
    f32_out = jax.ShapeDtypeStruct((bsz, seq, D_MODEL), F32)
    n_next = 0
    if reverse:
        n_next = w_next.shape[1] // D_MODEL
        once = pl.Buffered(1)
        in_specs = [main, gate_w, vec, vec, vec,
                    main, pl.BlockSpec((None, tile, D_MODEL), lambda b, i: (b, tix(i), 1)), main,
                    pl.BlockSpec((D_MODEL, D_MODEL), lambda b, i: (0, 0), pipeline_mode=once), vec,
                    pl.BlockSpec(w_next.shape, lambda b, i: (0, 0), pipeline_mode=once)]
        args = [xc] + gate_args + [hf, proj, x, w_out, g_next.reshape(1, D_MODEL), w_next]
        out_specs = [main, pl.BlockSpec((None, tile, w_next.shape[1]), lambda b, i: (b, tix(i), 0))]
        out_shape = [f32_out, jax.ShapeDtypeStruct((bsz, seq, w_next.shape[1]), BF16)]
    else:
        halo_p = pl.BlockSpec((None, BF16_ROWS, D_MODEL), lambda b, i: (b, jnp.maximum(i * per - 1, 0), 0))
        halo_n = pl.BlockSpec((None, BF16_ROWS, D_MODEL), lambda b, i: (b, jnp.minimum((i + 1) * per, last), 0))
        in_specs = [main, halo_p, halo_n, pl.BlockSpec((4, D_MODEL), lambda b, i: (0, 0)), vec,
                    gate_w, vec, vec, vec]
        args = [proj, proj, proj, conv_w, conv_b.reshape(1, D_MODEL)] + gate_args
        out_specs = [main, main]
        out_shape = [f32_out, f32_out]
    return pl.pallas_call(
        functools.partial(_rglru_kernel, tile=tile, n_tiles=n_tiles, reverse=reverse, n_next=n_next),
        grid=(bsz, n_tiles),
        in_specs=in_specs,
        out_specs=out_specs,
        out_shape=out_shape,
        scratch_shapes=[pltpu.VMEM((tile, D_MODEL), F32), pltpu.VMEM((tile, D_MODEL), F32),
                        pltpu.VMEM((1, D_MODEL), F32)],
        compiler_params=_params(("arbitrary", "arbitrary")),
        name="rglru_bwd_out" if reverse else "rglru_fwd",
    )(*args)


def _mixer_b(x, g, w_in, conv_w, conv_b, w_r, b_r, w_i, b_i, lam, w_out, g_next, w_next):
    bsz, seq, _ = x.shape
    proj = _norm_proj(x, g, w_in.astype(BF16)).reshape(bsz, seq, 2 * D_MODEL)
    wg = [jnp.concatenate([w_r[d], w_i[d]], axis=-1).astype(BF16) for d in range(2)]
    hf, xc = _rglru(proj, conv_w, conv_b, wg[0], b_r[0], b_i[0], lam[0], reverse=False)
    return _rglru(proj, conv_w, conv_b, wg[1], b_r[1], b_i[1], lam[1], reverse=True,
                  xc=xc, hf=hf, x=x, w_out=w_out.astype(BF16), g_next=g_next, w_next=w_next.astype(BF16))


def _split3(x):
    hi = x.astype(BF16)
    r1 = x - hi.astype(F32)
    mid = r1.astype(BF16)
    lo = (r1 - mid.astype(F32)).astype(BF16)
    return hi, mid, lo


def _hgrn_kernel(*refs, tile, reverse):
    if reverse:
        (q_ref, z_ref, v_ref, lb_ref, tri_ref, of_ref, gate_ref, x_ref, gn_ref, wo_ref,
         out_ref, st_scr, o_scr) = refs
    else:
        q_ref, z_ref, v_ref, lb_ref, tri_ref, out_ref, st_scr, o_scr = refs

    @pl.when(pl.program_id(1) == 0)
    def _():
        st_scr[...] = jnp.zeros_like(st_scr)

    n_blk = C_TILE // C_CHUNK
    order = list(reversed(range(n_blk))) if reverse else list(range(n_blk))
    mid_row = C_CHUNK // 2 - 1 if reverse else C_CHUNK // 2
    end_row = 0 if reverse else C_CHUNK - 1
    lb = lb_ref[...]
    one_m_lb = 1.0 - lb
    tri = tri_ref[...]

    n_sub = tile // C_TILE
    for sub in (reversed(range(n_sub)) if reverse else range(n_sub)):
        base = sub * C_TILE
        qt, kt, rho, blk_total = {}, {}, {}, {}
        for j in order:
            r0 = base + j * C_CHUNK
            z = z_ref[r0:r0 + C_CHUNK, :].astype(F32)
            half_t = 0.5 * jnp.tanh(0.5 * z)
            sig_pos = 0.5 + half_t
            sig_neg = 0.5 - half_t
            logf = jnp.log(lb + one_m_lb * sig_pos)
            b = _dot(tri, jnp.concatenate(_split3(logf), axis=0))
            b_mid = b[mid_row:mid_row + 1, :]
            qt[j] = q_ref[r0:r0 + C_CHUNK, :].astype(F32) * jnp.exp(b - b_mid)
            kt[j] = (one_m_lb * sig_neg) * jnp.exp(b_mid - b)
            rho[j] = b_mid
            blk_total[j] = b[end_row:end_row + 1, :]

        off = jnp.zeros_like(lb)
        for j in order:
            rho[j] = rho[j] + off
            off = off + blk_total[j]
        b_end = off
        decay = jnp.exp(b_end)

        qd = jnp.concatenate([(qt[j] * jnp.exp(rho[j])).astype(BF16) for j in range(n_blk)], axis=0)
        kd = jnp.concatenate([(kt[j] * jnp.exp(b_end - rho[j])).astype(BF16) for j in range(n_blk)], axis=0)

        zeros = jnp.zeros((C_CHUNK, D_MODEL), BF16)
        keys = {}
        for i in range(n_blk):
            parts = []
            for j in range(n_blk):
                if j == i:
                    parts.append(kt[j].astype(BF16))
                elif order.index(j) < order.index(i):
                    parts.append((kt[j] * jnp.exp(rho[i] - rho[j])).astype(BF16))
                else:
                    parts.append(zeros)
            keys[i] = jnp.concatenate(parts, axis=0)
        qt = {j: qt[j].astype(BF16) for j in qt}

        ti = lax.broadcasted_iota(jnp.int32, (C_TILE, C_TILE), 0)
        si = lax.broadcasted_iota(jnp.int32, (C_TILE, C_TILE), 1)
        visible = (si >= ti) if reverse else (si <= ti)
        zq = jnp.zeros((C_CHUNK, C_KDIM), BF16)
        rows = slice(base, base + C_TILE)
        for hd in range(C_HEADS):
            cs = slice(hd * C_KDIM, (hd + 1) * C_KDIM)
            q_diag = jnp.concatenate(
                [jnp.concatenate([qt[i][:, cs] if j == i else zq for j in range(n_blk)], axis=1)
                 for i in range(n_blk)], axis=0)
            k_cat = jnp.concatenate([keys[i][:, cs] for i in range(n_blk)], axis=1)
            att = jnp.where(visible, _dot_nt(q_diag, k_cat), 0.0).astype(BF16)
            st = st_scr[hd]
            o_scr[rows, cs] = _dot(att, v_ref[rows, cs]) + _dot_nt(qd[:, cs], st.astype(BF16))
            st_scr[hd] = decay[:, cs] * st + _dot_tn(v_ref[rows, cs], kd[:, cs])

    if reverse:
        gn = gn_ref[...]
        for hd in range(C_HEADS):
            c0 = hd * C_KDIM
            o = of_ref[:, c0:c0 + C_KDIM] + o_scr[:, c0:c0 + C_KDIM]
            o = o * lax.rsqrt(jnp.mean(o * o, axis=-1, keepdims=True) + EPS) * gn
            o_scr[:, c0:c0 + C_KDIM] = o
        z = (o_scr[...] * _silu(gate_ref[...].astype(F32))).astype(BF16)
        out_ref[...] = x_ref[...] + _dot(z, wo_ref[...])
    else:
        out_ref[...] = o_scr[...]


def _hgrn(proj, lb, *, reverse, of=None, x=None, gn=None, w_out=None, tile=512):
    bsz, seq, _ = proj.shape
    tile = min(tile, seq)
    n_tiles = seq // tile

    def tix(i):
        return (n_tiles - 1 - i) if reverse else i

    def col(j):
        return pl.BlockSpec((None, tile, D_MODEL), lambda b, i: (b, tix(i), j))

    ti = lax.broadcasted_iota(jnp.int32, (C_CHUNK, C_CHUNK), 0)
    si = lax.broadcasted_iota(jnp.int32, (C_CHUNK, C_CHUNK), 1)
    tri = ((si >= ti) if reverse else (si <= ti)).astype(BF16)
    tri = jnp.concatenate([tri, tri, tri], axis=1)
    in_specs = [col(0), col(2 if reverse else 1), col(3),
                pl.BlockSpec((1, D_MODEL), lambda b, i: (0, 0)),
                pl.BlockSpec((C_CHUNK, 3 * C_CHUNK), lambda b, i: (0, 0))]
    args = [proj, proj, proj, lb.reshape(1, D_MODEL), tri]
    if reverse:
        in_specs += [col(0), col(4), col(0),
                     pl.BlockSpec((1, C_KDIM), lambda b, i: (0, 0)),
                     pl.BlockSpec((D_MODEL, D_MODEL), lambda b, i: (0, 0))]
        args += [of, proj, x, gn.reshape(1, C_KDIM), w_out]
    return pl.pallas_call(
        functools.partial(_hgrn_kernel, tile=tile, reverse=reverse),
        grid=(bsz, n_tiles),
        in_specs=in_specs,
        out_specs=col(0),
        out_shape=jax.ShapeDtypeStruct((bsz, seq, D_MODEL), F32),
        scratch_shapes=[pltpu.VMEM((C_HEADS, C_KDIM, C_KDIM), F32), pltpu.VMEM((tile, D_MODEL), F32)],
        compiler_params=_params(("arbitrary", "arbitrary")),
        name="hgrn_bwd_out" if reverse else "hgrn_fwd",
    )(*args)


def _mixer_c(x, proj, lb, gn, w_out):
    of = _hgrn(proj, lb[0], reverse=False)
    return _hgrn(proj, lb[1], reverse=True, of=of, x=x, gn=gn, w_out=w_out.astype(BF16))


def _trunk(x, norm_g, final_g, a_w_in, a_w_out, b_w_in, b_conv_w, b_conv_b, b_w_r, b_b_r, b_w_i, b_b_i,
           b_lambda, b_w_out, c_w_in, c_lower_bounds, c_gnorm_g, c_w_out):
    depth = norm_g.shape[0]
    sm = jax.nn.softmax(c_lower_bounds.astype(F32), axis=0)
    lbs = jnp.cumsum(sm, axis=0) - sm[0]
    if depth % N_MIXERS != 1 or depth < 1:
        raise NotImplementedError("trunk depth must end on an attention layer")
    proj_c = None
    for layer in range(depth):
        kind = layer % N_MIXERS
        j = layer // N_MIXERS
        if kind == 0:
            x = _mixer_a(x, norm_g[layer], a_w_in[j], a_w_out[j], final_g if layer == depth - 1 else None)
        elif kind == 1:
            x, proj_c = _mixer_b(x, norm_g[layer], b_w_in[j], b_conv_w[j], b_conv_b[j], b_w_r[j], b_b_r[j],
                                 b_w_i[j], b_b_i[j], b_lambda[j], b_w_out[j], norm_g[layer + 1], c_w_in[j])
        else:
            x = _mixer_c(x, proj_c, lbs[layer], c_gnorm_g[j], c_w_out[j])
    return x


def kernel(x_prompt, x_sample, norm_g, final_g, a_w_in, a_w_out, b_w_in, b_conv_w, b_conv_b, b_w_r, b_b_r,
           b_w_i, b_b_i, b_lambda, b_w_out, c_w_in, c_lower_bounds, c_gnorm_g, c_w_out):
    weights = (norm_g, final_g, a_w_in, a_w_out, b_w_in, b_conv_w, b_conv_b, b_w_r, b_b_r, b_w_i, b_b_i,
               b_lambda, b_w_out, c_w_in, c_lower_bounds, c_gnorm_g, c_w_out)
    return (_trunk(x_prompt, *weights), _trunk(x_sample, *weights))
```

```python
import functools

import numpy as np
import jax
import jax.numpy as jnp
from jax import lax
from jax.experimental import pallas as pl
from jax.experimental.pallas import tpu as pltpu

F32 = jnp.float32
BF16 = jnp.bfloat16

D_MODEL = 1024
EPS = 1e-6
LANES = 128
SUBLANES = 8
BF16_ROWS = 16
VMEM_LIMIT = 56 * 1024 * 1024

N_MIXERS = 3
A_GROUPS = ((128, 1), (512, 4), (2048, 16))
A_HEADS = 8
A_HEAD_DIM = 128
A_WIDTH = A_HEADS * A_HEAD_DIM
A_HALF = 64
ROPE_THETA = 500000.0
ROPE_DIM = A_HEAD_DIM // 4
A_QBLK = 128
A_KBLK = A_QBLK + 2 * A_HALF
A_PERM = 256
PROJ_PIECE = 256
NEG = -1e30
LN2 = 0.6931471805599453
B_BLOCKS = 8
B_BLOCK_DIM = D_MODEL // B_BLOCKS
B_C = 8.0
C_HEADS = 8
C_KDIM = 128
C_CHUNK = 64
C_TILE = 256


def _params(semantics):
    return pltpu.CompilerParams(dimension_semantics=semantics, vmem_limit_bytes=VMEM_LIMIT)


def _rmsnorm(x, g):
    ms = jnp.mean(x * x, axis=-1, keepdims=True)
    return x * lax.rsqrt(ms + EPS) * g


def _sigmoid(x):
    return 0.5 * jnp.tanh(0.5 * x) + 0.5


def _silu(g):
    return g * _sigmoid(g)


def _dot(a, b):
    return jnp.dot(a, b, preferred_element_type=F32)


def _dot_nt(a, b):
    return lax.dot_general(a, b, (((1,), (1,)), ((), ())), preferred_element_type=F32)


def _dot_tn(a, b):
    return lax.dot_general(a, b, (((0,), (0,)), ((), ())), preferred_element_type=F32)


def _split2(x):
    hi = x.astype(BF16)
    return hi, (x - hi.astype(F32)).astype(BF16)


def _deinterleave_matrix(dil):
    n = A_PERM // dil
    idx = np.arange(A_PERM)
    p = np.zeros((A_PERM, A_PERM), np.float32)
    p[idx, (idx % n) * dil + idx // n] = 1.0
    return p


def _norm_proj_kernel(*refs, n_chunks, n_rope, q_scale, dil, tl):
    in_refs, o_ref = refs[:-1], refs[-1]
    x_ref, g_ref = in_refs[:2]
    w_refs = in_refs[2:2 + n_chunks]
    p_ref = in_refs[2 + n_chunks] if dil > 1 else None
    if n_rope:
        half = ROPE_DIM // 2
        t = in_refs[-1][...]
        lane = lax.broadcasted_iota(jnp.int32, t.shape, 1)
        lo_half, hi_half = lane < half, lane < ROPE_DIM
        ta = jnp.where(lo_half, t, jnp.where(hi_half, pltpu.roll(t, half, 1), 1.0))
        tb = jnp.where(lo_half, -pltpu.roll(t, LANES - half, 1), 0.0)
        tc = jnp.where(lo_half, 0.0, jnp.where(hi_half, t, 0.0))
    n_sub = tl // A_PERM
    n = A_PERM // dil

    h = _rmsnorm(x_ref[...], g_ref[...]).astype(BF16)
    if dil > 1:
        p = p_ref[...]
        h = jnp.concatenate(
            [_dot(p, h[s * A_PERM:(s + 1) * A_PERM, :]).astype(BF16) for s in range(n_sub)], axis=0)

    def store(val, lo, width):
        val = val.astype(o_ref.dtype)
        if dil == 1:
            o_ref[0, :, lo:lo + width] = val
        else:
            for s in range(n_sub):
                for r in range(dil):
                    a = s * A_PERM + r * n
                    o_ref[r, s * n:(s + 1) * n, lo:lo + width] = val[a:a + n, :]

    for lo in range(0, n_chunks * D_MODEL, PROJ_PIECE):
        acc = _dot(h, w_refs[lo // D_MODEL][:, lo % D_MODEL:lo % D_MODEL + PROJ_PIECE])
        if lo < n_rope * D_MODEL:
            for hd in range(PROJ_PIECE // LANES):
                s = acc[:, hd * LANES:(hd + 1) * LANES]
                r = (s * ta + pltpu.roll(s, LANES - ROPE_DIM // 2, 1) * tb
                     + pltpu.roll(s, ROPE_DIM // 2, 1) * tc)
                if lo < D_MODEL and q_scale is not None:
                    r = r * q_scale
                store(r, lo + hd * LANES, LANES)
        else:
            store(acc, lo, PROJ_PIECE)


def _norm_proj(x, g, w, layer, cols, *, dil=1, rope=None, n_rope=0, q_scale=None, tl=1024):
    bsz, seq, _ = x.shape
    n = len(cols) * D_MODEL
    tl = min(tl, seq)
    full = lambda b, i: (0, 0)
    in_specs = [
        pl.BlockSpec((None, tl, D_MODEL), lambda b, i: (b, i, 0)),
        pl.BlockSpec((1, D_MODEL), full),
    ]
    args = [x, g.reshape(1, D_MODEL)]
    for col in cols:
        in_specs.append(pl.BlockSpec((None, D_MODEL, D_MODEL), lambda b, i, col=col: (layer, 0, col)))
        args.append(w)
    if dil > 1:
        in_specs.append(pl.BlockSpec((A_PERM, A_PERM), full))
        args.append(jnp.asarray(_deinterleave_matrix(dil), BF16))
    if n_rope:
        in_specs.append(pl.BlockSpec((tl, LANES), lambda b, i: (i, 0)))
        args.append(rope)
    return pl.pallas_call(
        functools.partial(_norm_proj_kernel, n_chunks=n // D_MODEL, n_rope=n_rope, q_scale=q_scale,
                          dil=dil, tl=tl),
        grid=(bsz, seq // tl),
        in_specs=in_specs,
        out_specs=pl.BlockSpec((None, dil, tl // dil, n), lambda b, i: (b, 0, i, 0)),
        out_shape=jax.ShapeDtypeStruct((bsz, dil, seq // dil, n), BF16),
        compiler_params=_params(("arbitrary", "arbitrary")),
        name="norm_proj",
    )(*args)


def _attn_kernel(q_ref, k_ref, v_ref, kp_ref, kn_ref, vp_ref, vn_ref, o_ref, lse_ref, kext, vext, *, tq, rows):
    step = pl.program_id(1)
    l0 = (step % (rows // tq)) * tq
    first = (pl.program_id(0) == 0) & (step == 0)

    @pl.when(first)
    def _():
        vext[...] = jnp.ones_like(vext)

    kext[0:A_HALF, :] = kp_ref[...]
    kext[A_HALF:A_HALF + tq, :] = k_ref[...]
    kext[A_HALF + tq:, :] = kn_ref[...]
    for hd in range(A_HEADS):
        src = slice(hd * A_HEAD_DIM, (hd + 1) * A_HEAD_DIM)
        dst = slice(2 * hd * A_HEAD_DIM, (2 * hd + 1) * A_HEAD_DIM)
        vext[0:A_HALF, dst] = vp_ref[:, src]
        vext[A_HALF:A_HALF + tq, dst] = v_ref[:, src]
        vext[A_HALF + tq:, dst] = vn_ref[:, src]

    qi = lax.broadcasted_iota(jnp.int32, (A_QBLK, A_KBLK), 0)
    ci = lax.broadcasted_iota(jnp.int32, (A_QBLK, A_KBLK), 1)
    lane = lax.broadcasted_iota(jnp.int32, (A_QBLK, LANES), 1)
    rel = ci - qi
    for j in range(tq // A_QBLK):
        key = l0 + (j * A_QBLK - A_HALF) + ci
        bias = jnp.where(rel >= 0, jnp.where(rel <= 2 * A_HALF, 0.0, NEG), NEG)
        bias = jnp.where(key >= 0, jnp.where(key < rows, bias, NEG), NEG)
        r0 = j * A_QBLK
        lse_tile = jnp.zeros((A_QBLK, LANES), F32)
        for hd in range(A_HEADS):
            c0 = hd * A_HEAD_DIM
            q = q_ref[r0:r0 + A_QBLK, c0:c0 + A_HEAD_DIM]
            k = kext[r0:r0 + A_KBLK, c0:c0 + A_HEAD_DIM]
            v1 = vext[r0:r0 + A_KBLK, 2 * c0:2 * c0 + 2 * A_HEAD_DIM]
            s = _dot_nt(q, k) + bias
            m = jnp.max(s, axis=1, keepdims=True)
            p = jnp.exp2((s - m).astype(BF16))
            acc = _dot(p, v1)
            den = acc[:, A_HEAD_DIM:]
            o_ref[r0:r0 + A_QBLK, c0:c0 + A_HEAD_DIM] = (acc[:, :A_HEAD_DIM] / den).astype(o_ref.dtype)
            lse_tile = jnp.where(lane == hd, m * LN2 + jnp.log(den), lse_tile)
        lse_ref[r0:r0 + A_QBLK, :] = lse_tile


def _attn_group(qkv, *, tq=1024):
    bsz, dil, rows, _ = qkv.shape
    tq = min(tq, rows)
    nt = rows // tq
    nh = tq // A_HALF
    last = rows // A_HALF - 1

    def main(col, width=A_WIDTH):
        return pl.BlockSpec((None, None, tq, width), lambda b, n: (b, n // nt, n % nt, col))

    def halo_prev(col):
        return pl.BlockSpec((None, None, A_HALF, A_WIDTH),
                            lambda b, n: (b, n // nt, jnp.maximum((n % nt) * nh - 1, 0), col))

    def halo_next(col):
        return pl.BlockSpec((None, None, A_HALF, A_WIDTH),
                            lambda b, n: (b, n // nt, jnp.minimum((n % nt + 1) * nh, last), col))

    return pl.pallas_call(
        functools.partial(_attn_kernel, tq=tq, rows=rows),
        grid=(bsz, dil * nt),
        in_specs=[main(0), main(1), main(2), halo_prev(1), halo_next(1), halo_prev(2), halo_next(2)],
        out_specs=[main(0), main(0, LANES)],
        out_shape=[jax.ShapeDtypeStruct((bsz, dil, rows, A_WIDTH), BF16),
                   jax.ShapeDtypeStruct((bsz, dil, rows, LANES), F32)],
        scratch_shapes=[pltpu.VMEM((tq + 2 * A_HALF, A_WIDTH), BF16),
                        pltpu.VMEM((tq + 2 * A_HALF, 2 * A_WIDTH), BF16)],
        compiler_params=_params(("arbitrary", "arbitrary")),
        name="attn_group",
    )(qkv, qkv, qkv, qkv, qkv, qkv, qkv)


def _a_out_kernel(*refs, final, dils, tm):
    n_g = len(dils)
    x_ref = refs[0]
    o_refs = refs[1:1 + n_g]
    l_refs = refs[1 + n_g:1 + 2 * n_g]
    n_perm = sum(1 for d in dils if d > 1)
    p_refs = refs[1 + 2 * n_g:1 + 2 * n_g + n_perm]
    g_ref, wg_ref, wo_ref = refs[1 + 2 * n_g + n_perm:4 + 2 * n_g + n_perm]
    fg_ref = refs[-2] if final else None
    y_ref = refs[-1]

    for sub in range(tm // A_PERM):
        rows = slice(sub * A_PERM, (sub + 1) * A_PERM)

        def block(ref, dil):
            n = A_PERM // dil
            return jnp.concatenate([ref[r, sub * n:(sub + 1) * n, :] for r in range(dil)], axis=0)

        outs, lses = [], []
        pi = 0
        for gi, dil in enumerate(dils):
            o = block(o_refs[gi], dil)
            lse = block(l_refs[gi], dil)
            if dil > 1:
                pt = p_refs[pi][...]
                pi += 1
                o = _dot(pt, o)
                both = _dot(pt, jnp.concatenate(_split2(lse), axis=1))
                lse = both[:, :LANES] + both[:, LANES:]
            else:
                o = o.astype(F32)
            outs.append(o)
            lses.append(lse)

        mx = functools.reduce(jnp.maximum, lses)
        ws = [jnp.exp(l - mx) for l in lses]
        inv = 1.0 / functools.reduce(lambda a, b: a + b, ws)
        ws = [w * inv for w in ws]

        x = x_ref[rows, :]
        h = _rmsnorm(x, g_ref[...]).astype(BF16)
        gate = _silu(_dot(h, wg_ref[...]))
        cols = []
        for hd in range(A_HEADS):
            c0 = hd * A_HEAD_DIM
            o = outs[0][:, c0:c0 + A_HEAD_DIM] * ws[0][:, hd:hd + 1]
            for gi in range(1, n_g):
                o = o + outs[gi][:, c0:c0 + A_HEAD_DIM] * ws[gi][:, hd:hd + 1]
            cols.append((o * gate[:, c0:c0 + A_HEAD_DIM]).astype(BF16))
        y = x + _dot(jnp.concatenate(cols, axis=1), wo_ref[...])
        if final:
            y = _rmsnorm(y, fg_ref[...])
        y_ref[rows, :] = y


def _a_out(x, outs, lses, g, w_in, layer, gate_col, w_out, final_g, *, tm=1024):
    bsz, seq, _ = x.shape
    tm = min(tm, seq)
    final = final_g is not None
    dils = tuple(o.shape[1] for o in outs)
    tile = pl.BlockSpec((None, tm, D_MODEL), lambda b, i: (b, i, 0))
    vec = pl.BlockSpec((1, D_MODEL), lambda b, i: (0, 0))
    mat = pl.BlockSpec((D_MODEL, D_MODEL), lambda b, i: (0, 0))
    in_specs = [tile]
    args = [x]
    for width, arrs in ((A_WIDTH, outs), (LANES, lses)):
        for d, a in zip(dils, arrs):
            in_specs.append(pl.BlockSpec((None, d, tm // d, width), lambda b, i: (b, 0, i, 0)))
            args.append(a)
    for d in dils:
        if d > 1:
            in_specs.append(pl.BlockSpec((A_PERM, A_PERM), lambda b, i: (0, 0)))
            args.append(jnp.asarray(_deinterleave_matrix(d).T, BF16))
    in_specs += [vec, pl.BlockSpec((None, D_MODEL, D_MODEL), lambda b, i: (layer, 0, gate_col)), mat]
    args += [g.reshape(1, D_MODEL), w_in, w_out]
    if final:
        in_specs.append(vec)
        args.append(final_g.reshape(1, D_MODEL))
    return pl.pallas_call(
        functools.partial(_a_out_kernel, final=final, dils=dils, tm=tm),
        grid=(bsz, seq // tm),
        in_specs=in_specs,
        out_specs=tile,
        out_shape=jax.ShapeDtypeStruct((bsz, seq, D_MODEL), F32),
        compiler_params=_params(("arbitrary", "arbitrary")),
        name="a_out",
    )(*args)


def _rope_tables(seq, dil):
    half = ROPE_DIM // 2
    pos = jnp.arange(seq, dtype=F32)
    inv = ROPE_THETA ** (-(jnp.arange(half, dtype=F32) * 2.0) / ROPE_DIM)
    ang = pos[:, None] * inv[None, :]
    t = jnp.concatenate([jnp.cos(ang), jnp.sin(ang), jnp.zeros((seq, LANES - ROPE_DIM), F32)], axis=1)
    if dil > 1:
        t = t.reshape(seq // A_PERM, A_PERM // dil, dil, LANES)
        t = jnp.swapaxes(t, 1, 2).reshape(seq, LANES)
    return t


def _mixer_a(x, g, w_in, layer, w_out, final_g):
    seq = x.shape[1]
    n_grp = len(A_GROUPS)
    outs, lses = [], []
    for grp, (_, dil) in enumerate(A_GROUPS):
        qkv = _norm_proj(x, g, w_in, layer, [grp, n_grp + grp, 2 * n_grp + grp], dil=dil,
                         rope=_rope_tables(seq, dil), n_rope=2, q_scale=A_HEAD_DIM ** -0.5 / LN2)
        o, lse = _attn_group(qkv)
        outs.append(o)
        lses.append(lse)
    return _a_out(x, outs, lses, g, w_in, layer, 3 * n_grp, w_out.astype(BF16), final_g)


def _rglru_kernel(*refs, tile, n_tiles, reverse, n_next):
    step = pl.program_id(1)
    if reverse:
        (xc_ref, wg_ref, br_ref, bi_ref, lam_ref,
         hf_ref, gate_ref, x_ref, wo_ref, gnext_ref, wnext_ref, out_ref, pnext_ref,
         a_scr, u_scr, h_scr) = refs
        xc = xc_ref[...]
    else:
        (xm_ref, xp_ref, xn_ref, cw_ref, cb_ref, wg_ref, br_ref, bi_ref, lam_ref,
         out_ref, xc_ref, a_scr, u_scr, h_scr) = refs
        prev = jnp.where(step > 0, xp_ref[...].astype(F32), 0.0)
        nxt = jnp.where(step < n_tiles - 1, xn_ref[...].astype(F32), 0.0)
        xm = xm_ref[...].astype(F32)
        ext = jnp.concatenate([prev, xm, nxt], axis=0)
        n_ext = tile + 2 * BF16_ROWS
        xm1 = pltpu.roll(ext, 1, 0)[BF16_ROWS:BF16_ROWS + tile, :]
        xm2 = pltpu.roll(ext, 2, 0)[BF16_ROWS:BF16_ROWS + tile, :]
        xp1 = pltpu.roll(ext, n_ext - 1, 0)[BF16_ROWS:BF16_ROWS + tile, :]
        cw = cw_ref[...]
        xc = xm2 * cw[0:1, :] + xm1 * cw[1:2, :] + xm * cw[2:3, :] + xp1 * cw[3:4, :] + cb_ref[...]
        xc_ref[...] = xc

    lam = lam_ref[...]
    softplus = jnp.log1p(jnp.exp(-jnp.abs(lam))) + jnp.maximum(-lam, 0.0)
    for n in range(B_BLOCKS):
        c0 = n * B_BLOCK_DIM
        xcn = xc[:, c0:c0 + B_BLOCK_DIM]
        gates = _dot(xcn.astype(BF16), wg_ref[n])
        r = _sigmoid(gates[:, :B_BLOCK_DIM] + br_ref[:, c0:c0 + B_BLOCK_DIM])
        ig = _sigmoid(gates[:, B_BLOCK_DIM:] + bi_ref[:, c0:c0 + B_BLOCK_DIM])
        log_a = -B_C * r * softplus[:, c0:c0 + B_BLOCK_DIM]
        a = jnp.exp(log_a)
        mult = jnp.sqrt(-jnp.tanh(log_a) * (a * a + 1.0))
        a_scr[:, c0:c0 + B_BLOCK_DIM] = a
        u_scr[:, c0:c0 + B_BLOCK_DIM] = mult * (ig * xcn)

    @pl.when(step == 0)
    def _():
        h_scr[...] = jnp.zeros_like(h_scr)

    n_groups = tile // SUBLANES
    row8 = lax.broadcasted_iota(jnp.int32, (SUBLANES, D_MODEL), 0)

    def body(gi, h):
        g = (n_groups - 1 - gi) if reverse else gi
        off = pl.multiple_of(g * SUBLANES, SUBLANES)
        a8 = a_scr[pl.ds(off, SUBLANES), :]
        u8 = u_scr[pl.ds(off, SUBLANES), :]
        for s in (1, 2, 4):
            if reverse:
                a_sh = pltpu.roll(a8, SUBLANES - s, 0)
                u_sh = pltpu.roll(u8, SUBLANES - s, 0)
                keep = row8 < SUBLANES - s
            else:
                a_sh = pltpu.roll(a8, s, 0)
                u_sh = pltpu.roll(u8, s, 0)
                keep = row8 >= s
            u8 = jnp.where(keep, a8 * u_sh + u8, u8)
            a8 = jnp.where(keep, a8 * a_sh, a8)
        hs = a8 * h + u8
        u_scr[pl.ds(off, SUBLANES), :] = hs
        return hs[0:1, :] if reverse else hs[SUBLANES - 1:SUBLANES, :]

    h_scr[...] = lax.fori_loop(0, n_groups, body, h_scr[...], unroll=4)

    if reverse:
        y = hf_ref[...] + u_scr[...]
        z = (y * _silu(gate_ref[...].astype(F32))).astype(BF16)
        x_new = x_ref[...] + _dot(z, wo_ref[...])
        out_ref[...] = x_new
        h_next = _rmsnorm(x_new, gnext_ref[...]).astype(BF16)
        for c in range(n_next):
            cols = slice(c * D_MODEL, (c + 1) * D_MODEL)
            pnext_ref[:, cols] = _dot(h_next, wnext_ref[:, cols]).astype(pnext_ref.dtype)
    else:
        out_ref[...] = u_scr[...]


def _rglru(proj, conv_w, conv_b, wg, b_r, b_i, lam, *, reverse, xc=None, hf=None, x=None, w_out=None,
           g_next=None, w_next=None, tile=512):
    bsz, seq, _ = proj.shape
    tile = min(tile, seq)
    n_tiles = seq // tile
    per = tile // BF16_ROWS
    last = seq // BF16_ROWS - 1

    def tix(i):
        return (n_tiles - 1 - i) if reverse else i

    main = pl.BlockSpec((None, tile, D_MODEL), lambda b, i: (b, tix(i), 0))
    vec = pl.BlockSpec((1, D_MODEL), lambda b, i: (0, 0))
    gate_w = pl.BlockSpec((B_BLOCKS, B_BLOCK_DIM, 2 * B_BLOCK_DIM), lambda b, i: (0, 0, 0))
    gate_args = [wg, b_r.reshape(1, D_MODEL), b_i.reshape(1, D_MODEL), lam.reshape(1, D_MODEL)]
    f32_out = jax.ShapeDtypeStruct((bsz, seq, D_MODEL), F32)
    n_next = 0
    if reverse:
        layer_next, width_next = w_next[1], w_next[0].shape[2]
        n_next = width_next // D_MODEL
        once = pl.Buffered(1)
        in_specs = [main, gate_w, vec, vec, vec,
                    main, pl.BlockSpec((None, tile, D_MODEL), lambda b, i: (b, tix(i), 1)), main,
                    pl.BlockSpec((D_MODEL, D_MODEL), lambda b, i: (0, 0), pipeline_mode=once), vec,
                    pl.BlockSpec((None, D_MODEL, width_next), lambda b, i: (layer_next, 0, 0),
                                 pipeline_mode=once)]
        args = [xc] + gate_args + [hf, proj, x, w_out, g_next.reshape(1, D_MODEL), w_next[0]]
        out_specs = [main, pl.BlockSpec((None, tile, width_next), lambda b, i: (b, tix(i), 0))]
        out_shape = [f32_out, jax.ShapeDtypeStruct((bsz, seq, width_next), BF16)]
    else:
        halo_p = pl.BlockSpec((None, BF16_ROWS, D_MODEL), lambda b, i: (b, jnp.maximum(i * per - 1, 0), 0))
        halo_n = pl.BlockSpec((None, BF16_ROWS, D_MODEL), lambda b, i: (b, jnp.minimum((i + 1) * per, last), 0))
        in_specs = [main, halo_p, halo_n, pl.BlockSpec((4, D_MODEL), lambda b, i: (0, 0)), vec,
                    gate_w, vec, vec, vec]
        args = [proj, proj, proj, conv_w, conv_b.reshape(1, D_MODEL)] + gate_args
        out_specs = [main, main]
        out_shape = [f32_out, f32_out]
    return pl.pallas_call(
        functools.partial(_rglru_kernel, tile=tile, n_tiles=n_tiles, reverse=reverse, n_next=n_next),
        grid=(bsz, n_tiles),
        in_specs=in_specs,
        out_specs=out_specs,
        out_shape=out_shape,
        scratch_shapes=[pltpu.VMEM((tile, D_MODEL), F32), pltpu.VMEM((tile, D_MODEL), F32),
                        pltpu.VMEM((1, D_MODEL), F32)],
        compiler_params=_params(("arbitrary", "arbitrary")),
        name="rglru_bwd_out" if reverse else "rglru_fwd",
    )(*args)


def _mixer_b(x, g, w_in, layer, conv_w, conv_b, w_r, b_r, w_i, b_i, lam, w_out, g_next, w_next):
    bsz, seq, _ = x.shape
    proj = _norm_proj(x, g, w_in, layer, [0, 1]).reshape(bsz, seq, 2 * D_MODEL)
    wg = [jnp.concatenate([w_r[d], w_i[d]], axis=-1).astype(BF16) for d in range(2)]
    hf, xc = _rglru(proj, conv_w, conv_b, wg[0], b_r[0], b_i[0], lam[0], reverse=False)
    return _rglru(proj, conv_w, conv_b, wg[1], b_r[1], b_i[1], lam[1], reverse=True,
                  xc=xc, hf=hf, x=x, w_out=w_out.astype(BF16), g_next=g_next, w_next=(w_next, layer))


def _split3(x):
    hi = x.astype(BF16)
    r1 = x - hi.astype(F32)
    mid = r1.astype(BF16)
    lo = (r1 - mid.astype(F32)).astype(BF16)
    return hi, mid, lo


def _hgrn_kernel(*refs, tile, reverse):
    if reverse:
        (q_ref, z_ref, v_ref, lb_ref, tri_ref, of_ref, gate_ref, x_ref, gn_ref, wo_ref,
         out_ref, st_scr, o_scr) = refs
    else:
        q_ref, z_ref, v_ref, lb_ref, tri_ref, out_ref, st_scr, o_scr = refs

    @pl.when(pl.program_id(1) == 0)
    def _():
        st_scr[...] = jnp.zeros_like(st_scr)

    n_blk = C_TILE // C_CHUNK
    order = list(reversed(range(n_blk))) if reverse else list(range(n_blk))
    mid_row = C_CHUNK // 2 - 1 if reverse else C_CHUNK // 2
    end_row = 0 if reverse else C_CHUNK - 1
    lb = lb_ref[...]
    one_m_lb = 1.0 - lb
    tri = tri_ref[...]

    n_sub = tile // C_TILE
    for sub in (reversed(range(n_sub)) if reverse else range(n_sub)):
        base = sub * C_TILE
        qt, kt, rho, blk_total = {}, {}, {}, {}
        for j in order:
            r0 = base + j * C_CHUNK
            z = z_ref[r0:r0 + C_CHUNK, :].astype(F32)
            half_t = 0.5 * jnp.tanh(0.5 * z)
            sig_pos = 0.5 + half_t
            sig_neg = 0.5 - half_t
            logf = jnp.log(lb + one_m_lb * sig_pos)
            b = _dot(tri, jnp.concatenate(_split3(logf), axis=0))
            b_mid = b[mid_row:mid_row + 1, :]
            qt[j] = q_ref[r0:r0 + C_CHUNK, :].astype(F32) * jnp.exp(b - b_mid)
            kt[j] = (one_m_lb * sig_neg) * jnp.exp(b_mid - b)
            rho[j] = b_mid
            blk_total[j] = b[end_row:end_row + 1, :]

        off = jnp.zeros_like(lb)
        for j in order:
            rho[j] = rho[j] + off
            off = off + blk_total[j]
        b_end = off
        decay = jnp.exp(b_end)

        qd = jnp.concatenate([(qt[j] * jnp.exp(rho[j])).astype(BF16) for j in range(n_blk)], axis=0)
        kd = jnp.concatenate([(kt[j] * jnp.exp(b_end - rho[j])).astype(BF16) for j in range(n_blk)], axis=0)

        zeros = jnp.zeros((C_CHUNK, D_MODEL), BF16)
        keys = {}
        for i in range(n_blk):
            parts = []
            for j in range(n_blk):
                if j == i:
                    parts.append(kt[j].astype(BF16))
                elif order.index(j) < order.index(i):
                    parts.append((kt[j] * jnp.exp(rho[i] - rho[j])).astype(BF16))
                else:
                    parts.append(zeros)
            keys[i] = jnp.concatenate(parts, axis=0)
        qt = {j: qt[j].astype(BF16) for j in qt}

        ti = lax.broadcasted_iota(jnp.int32, (C_TILE, C_TILE), 0)
        si = lax.broadcasted_iota(jnp.int32, (C_TILE, C_TILE), 1)
        visible = (si >= ti) if reverse else (si <= ti)
        zq = jnp.zeros((C_CHUNK, C_KDIM), BF16)
        rows = slice(base, base + C_TILE)
        for hd in range(C_HEADS):
            cs = slice(hd * C_KDIM, (hd + 1) * C_KDIM)
            q_diag = jnp.concatenate(
                [jnp.concatenate([qt[i][:, cs] if j == i else zq for j in range(n_blk)], axis=1)
                 for i in range(n_blk)], axis=0)
            k_cat = jnp.concatenate([keys[i][:, cs] for i in range(n_blk)], axis=1)
            att = jnp.where(visible, _dot_nt(q_diag, k_cat), 0.0).astype(BF16)
            st = st_scr[hd]
            o_scr[rows, cs] = _dot(att, v_ref[rows, cs]) + _dot_nt(qd[:, cs], st.astype(BF16))
            st_scr[hd] = decay[:, cs] * st + _dot_tn(v_ref[rows, cs], kd[:, cs])

    if reverse:
        gn = gn_ref[...]
        for hd in range(C_HEADS):
            c0 = hd * C_KDIM
            o = of_ref[:, c0:c0 + C_KDIM] + o_scr[:, c0:c0 + C_KDIM]
            o = o * lax.rsqrt(jnp.mean(o * o, axis=-1, keepdims=True) + EPS) * gn
            o_scr[:, c0:c0 + C_KDIM] = o
        z = (o_scr[...] * _silu(gate_ref[...].astype(F32))).astype(BF16)
        out_ref[...] = x_ref[...] + _dot(z, wo_ref[...])
    else:
        out_ref[...] = o_scr[...]


def _hgrn(proj, lb, *, reverse, of=None, x=None, gn=None, w_out=None, tile=512):
    bsz, seq, _ = proj.shape
    tile = min(tile, seq)
    n_tiles = seq // tile

    def tix(i):
        return (n_tiles - 1 - i) if reverse else i

    def col(j):
        return pl.BlockSpec((None, tile, D_MODEL), lambda b, i: (b, tix(i), j))

    ti = lax.broadcasted_iota(jnp.int32, (C_CHUNK, C_CHUNK), 0)
    si = lax.broadcasted_iota(jnp.int32, (C_CHUNK, C_CHUNK), 1)
    tri = ((si >= ti) if reverse else (si <= ti)).astype(BF16)
    tri = jnp.concatenate([tri, tri, tri], axis=1)
    in_specs = [col(0), col(2 if reverse else 1), col(3),
                pl.BlockSpec((1, D_MODEL), lambda b, i: (0, 0)),
                pl.BlockSpec((C_CHUNK, 3 * C_CHUNK), lambda b, i: (0, 0))]
    args = [proj, proj, proj, lb.reshape(1, D_MODEL), tri]
    if reverse:
        in_specs += [col(0), col(4), col(0),
                     pl.BlockSpec((1, C_KDIM), lambda b, i: (0, 0)),
                     pl.BlockSpec((D_MODEL, D_MODEL), lambda b, i: (0, 0))]
        args += [of, proj, x, gn.reshape(1, C_KDIM), w_out]
    return pl.pallas_call(
        functools.partial(_hgrn_kernel, tile=tile, reverse=reverse),
        grid=(bsz, n_tiles),
        in_specs=in_specs,
        out_specs=col(0),
        out_shape=jax.ShapeDtypeStruct((bsz, seq, D_MODEL), F32),
        scratch_shapes=[pltpu.VMEM((C_HEADS, C_KDIM, C_KDIM), F32), pltpu.VMEM((tile, D_MODEL), F32)],
        compiler_params=_params(("arbitrary", "arbitrary")),
        name="hgrn_bwd_out" if reverse else "hgrn_fwd",
    )(*args)


def _mixer_c(x, proj, lb, gn, w_out):
    of = _hgrn(proj, lb[0], reverse=False)
    return _hgrn(proj, lb[1], reverse=True, of=of, x=x, gn=gn, w_out=w_out.astype(BF16))


def _trunk(x, norm_g, final_g, a_w_in, a_w_out, b_w_in, b_conv_w, b_conv_b, b_w_r, b_b_r, b_w_i, b_b_i,
           b_lambda, b_w_out, c_w_in, c_lower_bounds, c_gnorm_g, c_w_out):
    depth = norm_g.shape[0]
    sm = jax.nn.softmax(c_lower_bounds.astype(F32), axis=0)
    lbs = jnp.cumsum(sm, axis=0) - sm[0]
    if depth % N_MIXERS != 1 or depth < 1:
        raise NotImplementedError("trunk depth must end on an attention layer")
    proj_c = None
    for layer in range(depth):
        kind = layer % N_MIXERS
        j = layer // N_MIXERS
        if kind == 0:
            x = _mixer_a(x, norm_g[layer], a_w_in, j, a_w_out[j], final_g if layer == depth - 1 else None)
        elif kind == 1:
            x, proj_c = _mixer_b(x, norm_g[layer], b_w_in, j, b_conv_w[j], b_conv_b[j], b_w_r[j], b_b_r[j],
                                 b_w_i[j], b_b_i[j], b_lambda[j], b_w_out[j], norm_g[layer + 1], c_w_in)
        else:
            x = _mixer_c(x, proj_c, lbs[layer], c_gnorm_g[j], c_w_out[j])
    return x


def kernel(x_prompt, x_sample, norm_g, final_g, a_w_in, a_w_out, b_w_in, b_conv_w, b_conv_b, b_w_r, b_b_r,
           b_w_i, b_b_i, b_lambda, b_w_out, c_w_in, c_lower_bounds, c_gnorm_g, c_w_out):
    weights = (norm_g, final_g, a_w_in.astype(BF16), a_w_out, b_w_in.astype(BF16), b_conv_w, b_conv_b,
               b_w_r, b_b_r, b_w_i, b_b_i, b_lambda, b_w_out, c_w_in.astype(BF16), c_lower_bounds,
               c_gnorm_g, c_w_out)
    return (_trunk(x_prompt, *weights), _trunk(x_sample, *weights))
```

```python
import functools

import numpy as np
import jax
import jax.numpy as jnp
from jax import lax
from jax.experimental import pallas as pl
from jax.experimental.pallas import tpu as pltpu

F32 = jnp.float32
BF16 = jnp.bfloat16

D_MODEL = 1024
EPS = 1e-6
LANES = 128
SUBLANES = 8
BF16_ROWS = 16
VMEM_LIMIT = 56 * 1024 * 1024

N_MIXERS = 3
A_GROUPS = ((128, 1), (512, 4), (2048, 16))
A_HEADS = 8
A_HEAD_DIM = 128
A_WIDTH = A_HEADS * A_HEAD_DIM
A_HALF = 64
ROPE_THETA = 500000.0
ROPE_DIM = A_HEAD_DIM // 4
A_QBLK = 128
A_KBLK = A_QBLK + 2 * A_HALF
A_PERM = 256
PROJ_PIECE = 256
NEG = -1e30
LN2 = 0.6931471805599453
B_BLOCKS = 8
B_BLOCK_DIM = D_MODEL // B_BLOCKS
B_C = 8.0
C_HEADS = 8
C_KDIM = 128
C_CHUNK = 64
C_TILE = 256


def _params(semantics):
    return pltpu.CompilerParams(dimension_semantics=semantics, vmem_limit_bytes=VMEM_LIMIT)


def _rmsnorm(x, g):
    ms = jnp.mean(x * x, axis=-1, keepdims=True)
    return x * lax.rsqrt(ms + EPS) * g


def _sigmoid(x):
    return 0.5 * jnp.tanh(0.5 * x) + 0.5


def _silu(g):
    return g * _sigmoid(g)


def _dot(a, b):
    return jnp.dot(a, b, preferred_element_type=F32)


def _dot_nt(a, b):
    return lax.dot_general(a, b, (((1,), (1,)), ((), ())), preferred_element_type=F32)


def _dot_tn(a, b):
    return lax.dot_general(a, b, (((0,), (0,)), ((), ())), preferred_element_type=F32)


def _split2(x):
    hi = x.astype(BF16)
    return hi, (x - hi.astype(F32)).astype(BF16)


def _deinterleave_matrix(dil):
    n = A_PERM // dil
    idx = np.arange(A_PERM)
    p = np.zeros((A_PERM, A_PERM), np.float32)
    p[idx, (idx % n) * dil + idx // n] = 1.0
    return p


def _norm_proj_kernel(*refs, n_chunks, n_rope, q_scale, dil, tl):
    in_refs, o_ref = refs[:-1], refs[-1]
    x_ref, g_ref = in_refs[:2]
    w_refs = in_refs[2:2 + n_chunks]
    p_ref = in_refs[2 + n_chunks] if dil > 1 else None
    if n_rope:
        half = ROPE_DIM // 2
        t = in_refs[-1][...]
        lane = lax.broadcasted_iota(jnp.int32, t.shape, 1)
        lo_half, hi_half = lane < half, lane < ROPE_DIM
        ta = jnp.where(lo_half, t, jnp.where(hi_half, pltpu.roll(t, half, 1), 1.0))
        tb = jnp.where(lo_half, -pltpu.roll(t, LANES - half, 1), 0.0)
        tc = jnp.where(lo_half, 0.0, jnp.where(hi_half, t, 0.0))
    n_sub = tl // A_PERM
    n = A_PERM // dil

    h = _rmsnorm(x_ref[...], g_ref[...]).astype(BF16)
    if dil > 1:
        p = p_ref[...]
        h = jnp.concatenate(
            [_dot(p, h[s * A_PERM:(s + 1) * A_PERM, :]).astype(BF16) for s in range(n_sub)], axis=0)

    def store(val, lo, width):
        val = val.astype(o_ref.dtype)
        if dil == 1:
            o_ref[0, :, lo:lo + width] = val
        else:
            for s in range(n_sub):
                for r in range(dil):
                    a = s * A_PERM + r * n
                    o_ref[r, s * n:(s + 1) * n, lo:lo + width] = val[a:a + n, :]

    for lo in range(0, n_chunks * D_MODEL, PROJ_PIECE):
        acc = _dot(h, w_refs[lo // D_MODEL][:, lo % D_MODEL:lo % D_MODEL + PROJ_PIECE])
        if lo < n_rope * D_MODEL:
            for hd in range(PROJ_PIECE // LANES):
                s = acc[:, hd * LANES:(hd + 1) * LANES]
                r = (s * ta + pltpu.roll(s, LANES - ROPE_DIM // 2, 1) * tb
                     + pltpu.roll(s, ROPE_DIM // 2, 1) * tc)
                if lo < D_MODEL and q_scale is not None:
                    r = r * q_scale
                store(r, lo + hd * LANES, LANES)
        else:
            store(acc, lo, PROJ_PIECE)


def _norm_proj(x, g, w, layer, cols, *, dil=1, rope=None, n_rope=0, q_scale=None, tl=1024):
    bsz, seq, _ = x.shape
    n = len(cols) * D_MODEL
    tl = min(tl, seq)
    full = lambda b, i: (0, 0)
    in_specs = [
        pl.BlockSpec((None, tl, D_MODEL), lambda b, i: (b, i, 0)),
        pl.BlockSpec((1, D_MODEL), full),
    ]
    args = [x, g.reshape(1, D_MODEL)]
    for col in cols:
        in_specs.append(pl.BlockSpec((None, D_MODEL, D_MODEL), lambda b, i, col=col: (layer, 0, col)))
        args.append(w)
    if dil > 1:
        in_specs.append(pl.BlockSpec((A_PERM, A_PERM), full))
        args.append(jnp.asarray(_deinterleave_matrix(dil), BF16))
    if n_rope:
        in_specs.append(pl.BlockSpec((tl, LANES), lambda b, i: (i, 0)))
        args.append(rope)
    return pl.pallas_call(
        functools.partial(_norm_proj_kernel, n_chunks=n // D_MODEL, n_rope=n_rope, q_scale=q_scale,
                          dil=dil, tl=tl),
        grid=(bsz, seq // tl),
        in_specs=in_specs,
        out_specs=pl.BlockSpec((None, dil, tl // dil, n), lambda b, i: (b, 0, i, 0)),
        out_shape=jax.ShapeDtypeStruct((bsz, dil, seq // dil, n), BF16),
        compiler_params=_params(("arbitrary", "arbitrary")),
        name="norm_proj",
    )(*args)


def _attn_kernel(q_ref, k_ref, v_ref, kp_ref, kn_ref, vp_ref, vn_ref, o_ref, lse_ref, kext, vext, *, tq, rows):
    step = pl.program_id(1)
    l0 = (step % (rows // tq)) * tq
    first = (pl.program_id(0) == 0) & (step == 0)

    @pl.when(first)
    def _():
        vext[...] = jnp.ones_like(vext)

    kext[0:A_HALF, :] = kp_ref[...]
    kext[A_HALF:A_HALF + tq, :] = k_ref[...]
    kext[A_HALF + tq:, :] = kn_ref[...]
    for hd in range(A_HEADS):
        src = slice(hd * A_HEAD_DIM, (hd + 1) * A_HEAD_DIM)
        dst = slice(2 * hd * A_HEAD_DIM, (2 * hd + 1) * A_HEAD_DIM)
        vext[0:A_HALF, dst] = vp_ref[:, src]
        vext[A_HALF:A_HALF + tq, dst] = v_ref[:, src]
        vext[A_HALF + tq:, dst] = vn_ref[:, src]

    qi = lax.broadcasted_iota(jnp.int32, (A_QBLK, A_KBLK), 0)
    ci = lax.broadcasted_iota(jnp.int32, (A_QBLK, A_KBLK), 1)
    lane = lax.broadcasted_iota(jnp.int32, (A_QBLK, LANES), 1)
    rel = ci - qi
    for j in range(tq // A_QBLK):
        key = l0 + (j * A_QBLK - A_HALF) + ci
        bias = jnp.where(rel >= 0, jnp.where(rel <= 2 * A_HALF, 0.0, NEG), NEG)
        bias = jnp.where(key >= 0, jnp.where(key < rows, bias, NEG), NEG)
        r0 = j * A_QBLK
        lse_tile = jnp.zeros((A_QBLK, LANES), F32)
        for hd in range(A_HEADS):
            c0 = hd * A_HEAD_DIM
            q = q_ref[r0:r0 + A_QBLK, c0:c0 + A_HEAD_DIM]
            k = kext[r0:r0 + A_KBLK, c0:c0 + A_HEAD_DIM]
            v1 = vext[r0:r0 + A_KBLK, 2 * c0:2 * c0 + 2 * A_HEAD_DIM]
            s = _dot_nt(q, k) + bias
            m = jnp.max(s, axis=1, keepdims=True)
            p = jnp.exp2((s - m).astype(BF16))
            acc = _dot(p, v1)
            den = acc[:, A_HEAD_DIM:]
            o_ref[r0:r0 + A_QBLK, c0:c0 + A_HEAD_DIM] = (acc[:, :A_HEAD_DIM] / den).astype(o_ref.dtype)
            lse_tile = jnp.where(lane == hd, m * LN2 + jnp.log(den), lse_tile)
        lse_ref[r0:r0 + A_QBLK, :] = lse_tile


def _attn_group(qkv, *, tq=1024):
    bsz, dil, rows, _ = qkv.shape
    tq = min(tq, rows)
    nt = rows // tq
    nh = tq // A_HALF
    last = rows // A_HALF - 1

    def main(col, width=A_WIDTH):
        return pl.BlockSpec((None, None, tq, width), lambda b, n: (b, n // nt, n % nt, col))

    def halo_prev(col):
        return pl.BlockSpec((None, None, A_HALF, A_WIDTH),
                            lambda b, n: (b, n // nt, jnp.maximum((n % nt) * nh - 1, 0), col))

    def halo_next(col):
        return pl.BlockSpec((None, None, A_HALF, A_WIDTH),
                            lambda b, n: (b, n // nt, jnp.minimum((n % nt + 1) * nh, last), col))

    return pl.pallas_call(
        functools.partial(_attn_kernel, tq=tq, rows=rows),
        grid=(bsz, dil * nt),
        in_specs=[main(0), main(1), main(2), halo_prev(1), halo_next(1), halo_prev(2), halo_next(2)],
        out_specs=[main(0), main(0, LANES)],
        out_shape=[jax.ShapeDtypeStruct((bsz, dil, rows, A_WIDTH), BF16),
                   jax.ShapeDtypeStruct((bsz, dil, rows, LANES), F32)],
        scratch_shapes=[pltpu.VMEM((tq + 2 * A_HALF, A_WIDTH), BF16),
                        pltpu.VMEM((tq + 2 * A_HALF, 2 * A_WIDTH), BF16)],
        compiler_params=_params(("arbitrary", "arbitrary")),
        name="attn_group",
    )(qkv, qkv, qkv, qkv, qkv, qkv, qkv)


def _a_out_kernel(*refs, final, dils, tm):
    n_g = len(dils)
    x_ref = refs[0]
    o_refs = refs[1:1 + n_g]
    l_refs = refs[1 + n_g:1 + 2 * n_g]
    n_perm = sum(1 for d in dils if d > 1)
    p_refs = refs[1 + 2 * n_g:1 + 2 * n_g + n_perm]
    g_ref, wg_ref, wo_ref = refs[1 + 2 * n_g + n_perm:4 + 2 * n_g + n_perm]
    fg_ref = refs[-2] if final else None
    y_ref = refs[-1]

    for sub in range(tm // A_PERM):
        rows = slice(sub * A_PERM, (sub + 1) * A_PERM)

        def block(ref, dil):
            n = A_PERM // dil
            return jnp.concatenate([ref[r, sub * n:(sub + 1) * n, :] for r in range(dil)], axis=0)

        outs, lses = [], []
        pi = 0
        for gi, dil in enumerate(dils):
            o = block(o_refs[gi], dil)
            lse = block(l_refs[gi], dil)
            if dil > 1:
                pt = p_refs[pi][...]
                pi += 1
                o = _dot(pt, o)
                both = _dot(pt, jnp.concatenate(_split2(lse), axis=1))
                lse = both[:, :LANES] + both[:, LANES:]
            else:
                o = o.astype(F32)
            outs.append(o)
            lses.append(lse)

        mx = functools.reduce(jnp.maximum, lses)
        ws = [jnp.exp(l - mx) for l in lses]
        inv = 1.0 / functools.reduce(lambda a, b: a + b, ws)
        ws = [w * inv for w in ws]

        x = x_ref[rows, :]
        h = _rmsnorm(x, g_ref[...]).astype(BF16)
        gate = _silu(_dot(h, wg_ref[...]))
        cols = []
        for hd in range(A_HEADS):
            c0 = hd * A_HEAD_DIM
            o = outs[0][:, c0:c0 + A_HEAD_DIM] * ws[0][:, hd:hd + 1]
            for gi in range(1, n_g):
                o = o + outs[gi][:, c0:c0 + A_HEAD_DIM] * ws[gi][:, hd:hd + 1]
            cols.append((o * gate[:, c0:c0 + A_HEAD_DIM]).astype(BF16))
        y = x + _dot(jnp.concatenate(cols, axis=1), wo_ref[...])
        if final:
            y = _rmsnorm(y, fg_ref[...])
        y_ref[rows, :] = y


def _a_out(x, outs, lses, g, w_in, layer, gate_col, w_out, final_g, *, tm=1024):
    bsz, seq, _ = x.shape
    tm = min(tm, seq)
    final = final_g is not None
    dils = tuple(o.shape[1] for o in outs)
    tile = pl.BlockSpec((None, tm, D_MODEL), lambda b, i: (b, i, 0))
    vec = pl.BlockSpec((1, D_MODEL), lambda b, i: (0, 0))
    mat = pl.BlockSpec((D_MODEL, D_MODEL), lambda b, i: (0, 0))
    in_specs = [tile]
    args = [x]
    for width, arrs in ((A_WIDTH, outs), (LANES, lses)):
        for d, a in zip(dils, arrs):
            in_specs.append(pl.BlockSpec((None, d, tm // d, width), lambda b, i: (b, 0, i, 0)))
            args.append(a)
    for d in dils:
        if d > 1:
            in_specs.append(pl.BlockSpec((A_PERM, A_PERM), lambda b, i: (0, 0)))
            args.append(jnp.asarray(_deinterleave_matrix(d).T, BF16))
    in_specs += [vec, pl.BlockSpec((None, D_MODEL, D_MODEL), lambda b, i: (layer, 0, gate_col)), mat]
    args += [g.reshape(1, D_MODEL), w_in, w_out]
    if final:
        in_specs.append(vec)
        args.append(final_g.reshape(1, D_MODEL))
    return pl.pallas_call(
        functools.partial(_a_out_kernel, final=final, dils=dils, tm=tm),
        grid=(bsz, seq // tm),
        in_specs=in_specs,
        out_specs=tile,
        out_shape=jax.ShapeDtypeStruct((bsz, seq, D_MODEL), F32),
        compiler_params=_params(("arbitrary", "arbitrary")),
        name="a_out",
    )(*args)


def _rope_base(seq):
    half = ROPE_DIM // 2
    pos = jnp.arange(seq, dtype=F32)
    inv = ROPE_THETA ** (-(jnp.arange(half, dtype=F32) * 2.0) / ROPE_DIM)
    ang = pos[:, None] * inv[None, :]
    return jnp.concatenate([jnp.cos(ang), jnp.sin(ang), jnp.zeros((seq, LANES - ROPE_DIM), F32)], axis=1)


def _rope_tables(base, seq, dil):
    t = base[:seq]
    if dil > 1:
        t = t.reshape(seq // A_PERM, A_PERM // dil, dil, LANES)
        t = jnp.swapaxes(t, 1, 2).reshape(seq, LANES)
    return t


def _mixer_a(x, g, w_in, layer, w_out, final_g, rope_base):
    seq = x.shape[1]
    n_grp = len(A_GROUPS)
    outs, lses = [], []
    for grp, (_, dil) in enumerate(A_GROUPS):
        qkv = _norm_proj(x, g, w_in, layer, [grp, n_grp + grp, 2 * n_grp + grp], dil=dil,
                         rope=_rope_tables(rope_base, seq, dil), n_rope=2, q_scale=A_HEAD_DIM ** -0.5 / LN2)
        o, lse = _attn_group(qkv)
        outs.append(o)
        lses.append(lse)
    return _a_out(x, outs, lses, g, w_in, layer, 3 * n_grp, w_out.astype(BF16), final_g)


def _rglru_kernel(*refs, tile, n_tiles, reverse, n_next):
    step = pl.program_id(1)
    if reverse:
        (xc_ref, wg_ref, br_ref, bi_ref, lam_ref,
         hf_ref, gate_ref, x_ref, wo_ref, gnext_ref, wnext_ref, out_ref, pnext_ref,
         a_scr, u_scr, h_scr) = refs
        xc = xc_ref[...]
    else:
        (xm_ref, xp_ref, xn_ref, cw_ref, cb_ref, wg_ref, br_ref, bi_ref, lam_ref,
         out_ref, xc_ref, a_scr, u_scr, h_scr) = refs
        prev = jnp.where(step > 0, xp_ref[...].astype(F32), 0.0)
        nxt = jnp.where(step < n_tiles - 1, xn_ref[...].astype(F32), 0.0)
        xm = xm_ref[...].astype(F32)
        ext = jnp.concatenate([prev, xm, nxt], axis=0)
        n_ext = tile + 2 * BF16_ROWS
        xm1 = pltpu.roll(ext, 1, 0)[BF16_ROWS:BF16_ROWS + tile, :]
        xm2 = pltpu.roll(ext, 2, 0)[BF16_ROWS:BF16_ROWS + tile, :]
        xp1 = pltpu.roll(ext, n_ext - 1, 0)[BF16_ROWS:BF16_ROWS + tile, :]
        cw = cw_ref[...]
        xc = xm2 * cw[0:1, :] + xm1 * cw[1:2, :] + xm * cw[2:3, :] + xp1 * cw[3:4, :] + cb_ref[...]
        xc_ref[...] = xc

    lam = lam_ref[...]
    softplus = jnp.log1p(jnp.exp(-jnp.abs(lam))) + jnp.maximum(-lam, 0.0)
    for n in range(B_BLOCKS):
        c0 = n * B_BLOCK_DIM
        xcn = xc[:, c0:c0 + B_BLOCK_DIM]
        gates = _dot(xcn.astype(BF16), wg_ref[n])
        r = _sigmoid(gates[:, :B_BLOCK_DIM] + br_ref[:, c0:c0 + B_BLOCK_DIM])
        ig = _sigmoid(gates[:, B_BLOCK_DIM:] + bi_ref[:, c0:c0 + B_BLOCK_DIM])
        log_a = -B_C * r * softplus[:, c0:c0 + B_BLOCK_DIM]
        a = jnp.exp(log_a)
        mult = jnp.sqrt(-jnp.tanh(log_a) * (a * a + 1.0))
        a_scr[:, c0:c0 + B_BLOCK_DIM] = a
        u_scr[:, c0:c0 + B_BLOCK_DIM] = mult * (ig * xcn)

    @pl.when(step == 0)
    def _():
        h_scr[...] = jnp.zeros_like(h_scr)

    n_groups = tile // SUBLANES
    row8 = lax.broadcasted_iota(jnp.int32, (SUBLANES, D_MODEL), 0)

    def body(gi, h):
        g = (n_groups - 1 - gi) if reverse else gi
        off = pl.multiple_of(g * SUBLANES, SUBLANES)
        a8 = a_scr[pl.ds(off, SUBLANES), :]
        u8 = u_scr[pl.ds(off, SUBLANES), :]
        for s in (1, 2, 4):
            if reverse:
                a_sh = pltpu.roll(a8, SUBLANES - s, 0)
                u_sh = pltpu.roll(u8, SUBLANES - s, 0)
                keep = row8 < SUBLANES - s
            else:
                a_sh = pltpu.roll(a8, s, 0)
                u_sh = pltpu.roll(u8, s, 0)
                keep = row8 >= s
            u8 = jnp.where(keep, a8 * u_sh + u8, u8)
            a8 = jnp.where(keep, a8 * a_sh, a8)
        hs = a8 * h + u8
        u_scr[pl.ds(off, SUBLANES), :] = hs
        return hs[0:1, :] if reverse else hs[SUBLANES - 1:SUBLANES, :]

    h_scr[...] = lax.fori_loop(0, n_groups, body, h_scr[...], unroll=4)

    if reverse:
        y = hf_ref[...] + u_scr[...]
        z = (y * _silu(gate_ref[...].astype(F32))).astype(BF16)
        x_new = x_ref[...] + _dot(z, wo_ref[...])
        out_ref[...] = x_new
        h_next = _rmsnorm(x_new, gnext_ref[...]).astype(BF16)
        for c in range(n_next):
            cols = slice(c * D_MODEL, (c + 1) * D_MODEL)
            pnext_ref[:, cols] = _dot(h_next, wnext_ref[:, cols]).astype(pnext_ref.dtype)
    else:
        out_ref[...] = u_scr[...]


def _rglru(proj, conv_w, conv_b, wg, b_r, b_i, lam, *, reverse, xc=None, hf=None, x=None, w_out=None,
           g_next=None, w_next=None, tile=512):
    bsz, seq, _ = proj.shape
    tile = min(tile, seq)
    n_tiles = seq // tile
    per = tile // BF16_ROWS
    last = seq // BF16_ROWS - 1

    def tix(i):
        return (n_tiles - 1 - i) if reverse else i

    main = pl.BlockSpec((None, tile, D_MODEL), lambda b, i: (b, tix(i), 0))
    vec = pl.BlockSpec((1, D_MODEL), lambda b, i: (0, 0))
    gate_w = pl.BlockSpec((B_BLOCKS, B_BLOCK_DIM, 2 * B_BLOCK_DIM), lambda b, i: (0, 0, 0))
    gate_args = [wg, b_r.reshape(1, D_MODEL), b_i.reshape(1, D_MODEL), lam.reshape(1, D_MODEL)]
    f32_out = jax.ShapeDtypeStruct((bsz, seq, D_MODEL), F32)
    n_next = 0
    if reverse:
        layer_next, width_next = w_next[1], w_next[0].shape[2]
        n_next = width_next // D_MODEL
        once = pl.Buffered(1)
        in_specs = [main, gate_w, vec, vec, vec,
                    main, pl.BlockSpec((None, tile, D_MODEL), lambda b, i: (b, tix(i), 1)), main,
                    pl.BlockSpec((D_MODEL, D_MODEL), lambda b, i: (0, 0), pipeline_mode=once), vec,
                    pl.BlockSpec((None, D_MODEL, width_next), lambda b, i: (layer_next, 0, 0),
                                 pipeline_mode=once)]
        args = [xc] + gate_args + [hf, proj, x, w_out, g_next.reshape(1, D_MODEL), w_next[0]]
        out_specs = [main, pl.BlockSpec((None, tile, width_next), lambda b, i: (b, tix(i), 0))]
        out_shape = [f32_out, jax.ShapeDtypeStruct((bsz, seq, width_next), BF16)]
    else:
        halo_p = pl.BlockSpec((None, BF16_ROWS, D_MODEL), lambda b, i: (b, jnp.maximum(i * per - 1, 0), 0))
        halo_n = pl.BlockSpec((None, BF16_ROWS, D_MODEL), lambda b, i: (b, jnp.minimum((i + 1) * per, last), 0))
        in_specs = [main, halo_p, halo_n, pl.BlockSpec((4, D_MODEL), lambda b, i: (0, 0)), vec,
                    gate_w, vec, vec, vec]
        args = [proj, proj, proj, conv_w, conv_b.reshape(1, D_MODEL)] + gate_args
        out_specs = [main, main]
        out_shape = [f32_out, f32_out]
    return pl.pallas_call(
        functools.partial(_rglru_kernel, tile=tile, n_tiles=n_tiles, reverse=reverse, n_next=n_next),
        grid=(bsz, n_tiles),
        in_specs=in_specs,
        out_specs=out_specs,
        out_shape=out_shape,
        scratch_shapes=[pltpu.VMEM((tile, D_MODEL), F32), pltpu.VMEM((tile, D_MODEL), F32),
                        pltpu.VMEM((1, D_MODEL), F32)],
        compiler_params=_params(("arbitrary", "arbitrary")),
        name="rglru_bwd_out" if reverse else "rglru_fwd",
    )(*args)


def _mixer_b(x, g, w_in, layer, conv_w, conv_b, w_r, b_r, w_i, b_i, lam, w_out, g_next, w_next):
    bsz, seq, _ = x.shape
    proj = _norm_proj(x, g, w_in, layer, [0, 1]).reshape(bsz, seq, 2 * D_MODEL)
    wg = [jnp.concatenate([w_r[d], w_i[d]], axis=-1).astype(BF16) for d in range(2)]
    hf, xc = _rglru(proj, conv_w, conv_b, wg[0], b_r[0], b_i[0], lam[0], reverse=False)
    return _rglru(proj, conv_w, conv_b, wg[1], b_r[1], b_i[1], lam[1], reverse=True,
                  xc=xc, hf=hf, x=x, w_out=w_out.astype(BF16), g_next=g_next, w_next=(w_next, layer))


def _split3(x):
    hi = x.astype(BF16)
    r1 = x - hi.astype(F32)
    mid = r1.astype(BF16)
    lo = (r1 - mid.astype(F32)).astype(BF16)
    return hi, mid, lo


def _hgrn_kernel(*refs, tile, reverse):
    if reverse:
        (q_ref, z_ref, v_ref, lb_ref, tri_ref, of_ref, gate_ref, x_ref, gn_ref, wo_ref,
         out_ref, st_scr, o_scr) = refs
    else:
        q_ref, z_ref, v_ref, lb_ref, tri_ref, out_ref, st_scr, o_scr = refs

    @pl.when(pl.program_id(1) == 0)
    def _():
        st_scr[...] = jnp.zeros_like(st_scr)

    n_blk = C_TILE // C_CHUNK
    order = list(reversed(range(n_blk))) if reverse else list(range(n_blk))
    mid_row = C_CHUNK // 2 - 1 if reverse else C_CHUNK // 2
    end_row = 0 if reverse else C_CHUNK - 1
    lb = lb_ref[...]
    one_m_lb = 1.0 - lb
    tri = tri_ref[...]

    n_sub = tile // C_TILE
    for sub in (reversed(range(n_sub)) if reverse else range(n_sub)):
        base = sub * C_TILE
        qt, kt, rho, blk_total = {}, {}, {}, {}
        for j in order:
            r0 = base + j * C_CHUNK
            z = z_ref[r0:r0 + C_CHUNK, :].astype(F32)
            half_t = 0.5 * jnp.tanh(0.5 * z)
            sig_pos = 0.5 + half_t
            sig_neg = 0.5 - half_t
            logf = jnp.log(lb + one_m_lb * sig_pos)
            b = _dot(tri, jnp.concatenate(_split3(logf), axis=0))
            b_mid = b[mid_row:mid_row + 1, :]
            qt[j] = q_ref[r0:r0 + C_CHUNK, :].astype(F32) * jnp.exp(b - b_mid)
            kt[j] = (one_m_lb * sig_neg) * jnp.exp(b_mid - b)
            rho[j] = b_mid
            blk_total[j] = b[end_row:end_row + 1, :]

        off = jnp.zeros_like(lb)
        for j in order:
            rho[j] = rho[j] + off
            off = off + blk_total[j]
        b_end = off
        decay = jnp.exp(b_end)

        qd = jnp.concatenate([(qt[j] * jnp.exp(rho[j])).astype(BF16) for j in range(n_blk)], axis=0)
        kd = jnp.concatenate([(kt[j] * jnp.exp(b_end - rho[j])).astype(BF16) for j in range(n_blk)], axis=0)

        zeros = jnp.zeros((C_CHUNK, D_MODEL), BF16)
        keys = {}
        for i in range(n_blk):
            parts = []
            for j in range(n_blk):
                if j == i:
                    parts.append(kt[j].astype(BF16))
                elif order.index(j) < order.index(i):
                    parts.append((kt[j] * jnp.exp(rho[i] - rho[j])).astype(BF16))
                else:
                    parts.append(zeros)
            keys[i] = jnp.concatenate(parts, axis=0)
        qt = {j: qt[j].astype(BF16) for j in qt}

        ti = lax.broadcasted_iota(jnp.int32, (C_TILE, C_TILE), 0)
        si = lax.broadcasted_iota(jnp.int32, (C_TILE, C_TILE), 1)
        visible = (si >= ti) if reverse else (si <= ti)
        zq = jnp.zeros((C_CHUNK, C_KDIM), BF16)
        rows = slice(base, base + C_TILE)
        for hd in range(C_HEADS):
            cs = slice(hd * C_KDIM, (hd + 1) * C_KDIM)
            q_diag = jnp.concatenate(
                [jnp.concatenate([qt[i][:, cs] if j == i else zq for j in range(n_blk)], axis=1)
                 for i in range(n_blk)], axis=0)
            k_cat = jnp.concatenate([keys[i][:, cs] for i in range(n_blk)], axis=1)
            att = jnp.where(visible, _dot_nt(q_diag, k_cat), 0.0).astype(BF16)
            st = st_scr[hd]
            o_scr[rows, cs] = _dot(att, v_ref[rows, cs]) + _dot_nt(qd[:, cs], st.astype(BF16))
            st_scr[hd] = decay[:, cs] * st + _dot_tn(v_ref[rows, cs], kd[:, cs])

    if reverse:
        gn = gn_ref[...]
        for hd in range(C_HEADS):
            c0 = hd * C_KDIM
            o = of_ref[:, c0:c0 + C_KDIM] + o_scr[:, c0:c0 + C_KDIM]
            o = o * lax.rsqrt(jnp.mean(o * o, axis=-1, keepdims=True) + EPS) * gn
            o_scr[:, c0:c0 + C_KDIM] = o
        z = (o_scr[...] * _silu(gate_ref[...].astype(F32))).astype(BF16)
        out_ref[...] = x_ref[...] + _dot(z, wo_ref[...])
    else:
        out_ref[...] = o_scr[...]


def _hgrn(proj, lb, *, reverse, of=None, x=None, gn=None, w_out=None, tile=512):
    bsz, seq, _ = proj.shape
    tile = min(tile, seq)
    n_tiles = seq // tile

    def tix(i):
        return (n_tiles - 1 - i) if reverse else i

    def col(j):
        return pl.BlockSpec((None, tile, D_MODEL), lambda b, i: (b, tix(i), j))

    ti = lax.broadcasted_iota(jnp.int32, (C_CHUNK, C_CHUNK), 0)
    si = lax.broadcasted_iota(jnp.int32, (C_CHUNK, C_CHUNK), 1)
    tri = ((si >= ti) if reverse else (si <= ti)).astype(BF16)
    tri = jnp.concatenate([tri, tri, tri], axis=1)
    in_specs = [col(0), col(2 if reverse else 1), col(3),
                pl.BlockSpec((1, D_MODEL), lambda b, i: (0, 0)),
                pl.BlockSpec((C_CHUNK, 3 * C_CHUNK), lambda b, i: (0, 0))]
    args = [proj, proj, proj, lb.reshape(1, D_MODEL), tri]
    if reverse:
        in_specs += [col(0), col(4), col(0),
                     pl.BlockSpec((1, C_KDIM), lambda b, i: (0, 0)),
                     pl.BlockSpec((D_MODEL, D_MODEL), lambda b, i: (0, 0))]
        args += [of, proj, x, gn.reshape(1, C_KDIM), w_out]
    return pl.pallas_call(
        functools.partial(_hgrn_kernel, tile=tile, reverse=reverse),
        grid=(bsz, n_tiles),
        in_specs=in_specs,
        out_specs=col(0),
        out_shape=jax.ShapeDtypeStruct((bsz, seq, D_MODEL), F32),
        scratch_shapes=[pltpu.VMEM((C_HEADS, C_KDIM, C_KDIM), F32), pltpu.VMEM((tile, D_MODEL), F32)],
        compiler_params=_params(("arbitrary", "arbitrary")),
        name="hgrn_bwd_out" if reverse else "hgrn_fwd",
    )(*args)


def _mixer_c(x, proj, lb, gn, w_out):
    of = _hgrn(proj, lb[0], reverse=False)
    return _hgrn(proj, lb[1], reverse=True, of=of, x=x, gn=gn, w_out=w_out.astype(BF16))


def _trunk(x, rope_base, norm_g, final_g, a_w_in, a_w_out, b_w_in, b_conv_w, b_conv_b, b_w_r, b_b_r, b_w_i, b_b_i,
           b_lambda, b_w_out, c_w_in, c_lower_bounds, c_gnorm_g, c_w_out):
    depth = norm_g.shape[0]
    sm = jax.nn.softmax(c_lower_bounds.astype(F32), axis=0)
    lbs = jnp.cumsum(sm, axis=0) - sm[0]
    if depth % N_MIXERS != 1 or depth < 1:
        raise NotImplementedError("trunk depth must end on an attention layer")
    proj_c = None
    for layer in range(depth):
        kind = layer % N_MIXERS
        j = layer // N_MIXERS
        if kind == 0:
            x = _mixer_a(x, norm_g[layer], a_w_in, j, a_w_out[j], final_g if layer == depth - 1 else None,
                         rope_base)
        elif kind == 1:
            x, proj_c = _mixer_b(x, norm_g[layer], b_w_in, j, b_conv_w[j], b_conv_b[j], b_w_r[j], b_b_r[j],
                                 b_w_i[j], b_b_i[j], b_lambda[j], b_w_out[j], norm_g[layer + 1], c_w_in)
        else:
            x = _mixer_c(x, proj_c, lbs[layer], c_gnorm_g[j], c_w_out[j])
    return x


def kernel(x_prompt, x_sample, norm_g, final_g, a_w_in, a_w_out, b_w_in, b_conv_w, b_conv_b, b_w_r, b_b_r,
           b_w_i, b_b_i, b_lambda, b_w_out, c_w_in, c_lower_bounds, c_gnorm_g, c_w_out):
    weights = (norm_g, final_g, a_w_in.astype(BF16), a_w_out, b_w_in.astype(BF16), b_conv_w, b_conv_b,
               b_w_r, b_b_r, b_w_i, b_b_i, b_lambda, b_w_out, c_w_in.astype(BF16), c_lower_bounds,
               c_gnorm_g, c_w_out)
    rope_base = _rope_base(max(x_prompt.shape[1], x_sample.shape[1]))
    return (_trunk(x_prompt, rope_base, *weights), _trunk(x_sample, rope_base, *weights))
```

```python
import functools

import numpy as np
import jax
import jax.numpy as jnp
from jax import lax
from jax.experimental import pallas as pl
from jax.experimental.pallas import tpu as pltpu

F32 = jnp.float32
BF16 = jnp.bfloat16

D_MODEL = 1024
EPS = 1e-6
LANES = 128
SUBLANES = 8
BF16_ROWS = 16
VMEM_LIMIT = 56 * 1024 * 1024

N_MIXERS = 3
A_GROUPS = ((128, 1), (512, 4), (2048, 16))
A_HEADS = 8
A_HEAD_DIM = 128
A_WIDTH = A_HEADS * A_HEAD_DIM
A_HALF = 64
ROPE_THETA = 500000.0
ROPE_DIM = A_HEAD_DIM // 4
A_QBLK = 128
A_KBLK = A_QBLK + 2 * A_HALF
A_PERM = 256
PROJ_PIECE = 256
NEG = -1e30
LN2 = 0.6931471805599453
B_BLOCKS = 8
B_BLOCK_DIM = D_MODEL // B_BLOCKS
B_C = 8.0
C_HEADS = 8
C_KDIM = 128
C_CHUNK = 64
C_TILE = 256


def _params(semantics):
    return pltpu.CompilerParams(dimension_semantics=semantics, vmem_limit_bytes=VMEM_LIMIT)


def _rmsnorm(x, g):
    ms = jnp.mean(x * x, axis=-1, keepdims=True)
    return x * lax.rsqrt(ms + EPS) * g


def _sigmoid(x):
    return 0.5 * jnp.tanh(0.5 * x) + 0.5


def _silu(g):
    return g * _sigmoid(g)


def _dot(a, b):
    return jnp.dot(a, b, preferred_element_type=F32)


def _dot_nt(a, b):
    return lax.dot_general(a, b, (((1,), (1,)), ((), ())), preferred_element_type=F32)


def _dot_tn(a, b):
    return lax.dot_general(a, b, (((0,), (0,)), ((), ())), preferred_element_type=F32)


def _split2(x):
    hi = x.astype(BF16)
    return hi, (x - hi.astype(F32)).astype(BF16)


def _deinterleave_matrix(dil):
    n = A_PERM // dil
    idx = np.arange(A_PERM)
    p = np.zeros((A_PERM, A_PERM), np.float32)
    p[idx, (idx % n) * dil + idx // n] = 1.0
    return p


def _norm_proj_kernel(*refs, n_chunks, n_rope, q_scale, dil, tl):
    in_refs, o_ref = refs[:-1], refs[-1]
    x_ref, g_ref = in_refs[:2]
    w_refs = in_refs[2:2 + n_chunks]
    p_ref = in_refs[2 + n_chunks] if dil > 1 else None
    if n_rope:
        half = ROPE_DIM // 2
        t = in_refs[-1][...]
        lane = lax.broadcasted_iota(jnp.int32, t.shape, 1)
        lo_half, hi_half = lane < half, lane < ROPE_DIM
        ta = jnp.where(lo_half, t, jnp.where(hi_half, pltpu.roll(t, half, 1), 1.0))
        tb = jnp.where(lo_half, -pltpu.roll(t, LANES - half, 1), 0.0)
        tc = jnp.where(lo_half, 0.0, jnp.where(hi_half, t, 0.0))
    n_sub = tl // A_PERM
    n = A_PERM // dil

    h = _rmsnorm(x_ref[...], g_ref[...]).astype(BF16)
    if dil > 1:
        p = p_ref[...]
        h = jnp.concatenate(
            [_dot(p, h[s * A_PERM:(s + 1) * A_PERM, :]).astype(BF16) for s in range(n_sub)], axis=0)

    def store(val, lo, width):
        val = val.astype(o_ref.dtype)
        if dil == 1:
            o_ref[0, :, lo:lo + width] = val
        else:
            for s in range(n_sub):
                for r in range(dil):
                    a = s * A_PERM + r * n
                    o_ref[r, s * n:(s + 1) * n, lo:lo + width] = val[a:a + n, :]

    for lo in range(0, n_chunks * D_MODEL, PROJ_PIECE):
        acc = _dot(h, w_refs[lo // D_MODEL][:, lo % D_MODEL:lo % D_MODEL + PROJ_PIECE])
        if lo < n_rope * D_MODEL:
            for hd in range(PROJ_PIECE // LANES):
                s = acc[:, hd * LANES:(hd + 1) * LANES]
                r = (s * ta + pltpu.roll(s, LANES - ROPE_DIM // 2, 1) * tb
                     + pltpu.roll(s, ROPE_DIM // 2, 1) * tc)
                if lo < D_MODEL and q_scale is not None:
                    r = r * q_scale
                store(r, lo + hd * LANES, LANES)
        else:
            store(acc, lo, PROJ_PIECE)


def _norm_proj(x, g, w, layer, cols, *, dil=1, rope=None, n_rope=0, q_scale=None, tl=1024):
    bsz, seq, _ = x.shape
    n = len(cols) * D_MODEL
    tl = min(tl, seq)
    full = lambda b, i: (0, 0)
    in_specs = [
        pl.BlockSpec((None, tl, D_MODEL), lambda b, i: (b, i, 0)),
        pl.BlockSpec((1, D_MODEL), full),
    ]
    args = [x, g.reshape(1, D_MODEL)]
    for col in cols:
        in_specs.append(pl.BlockSpec((None, D_MODEL, D_MODEL), lambda b, i, col=col: (layer, 0, col)))
        args.append(w)
    if dil > 1:
        in_specs.append(pl.BlockSpec((A_PERM, A_PERM), full))
        args.append(jnp.asarray(_deinterleave_matrix(dil), BF16))
    if n_rope:
        in_specs.append(pl.BlockSpec((tl, LANES), lambda b, i: (i, 0)))
        args.append(rope)
    return pl.pallas_call(
        functools.partial(_norm_proj_kernel, n_chunks=n // D_MODEL, n_rope=n_rope, q_scale=q_scale,
                          dil=dil, tl=tl),
        grid=(bsz, seq // tl),
        in_specs=in_specs,
        out_specs=pl.BlockSpec((None, dil, tl // dil, n), lambda b, i: (b, 0, i, 0)),
        out_shape=jax.ShapeDtypeStruct((bsz, dil, seq // dil, n), BF16),
        compiler_params=_params(("arbitrary", "arbitrary")),
        name="norm_proj",
    )(*args)


def _attn_kernel(q_ref, k_ref, v_ref, kp_ref, kn_ref, vp_ref, vn_ref, o_ref, lse_ref, kext, vext, *, tq, rows):
    step = pl.program_id(1)
    l0 = (step % (rows // tq)) * tq
    first = (pl.program_id(0) == 0) & (step == 0)

    @pl.when(first)
    def _():
        vext[...] = jnp.ones_like(vext)

    kext[0:A_HALF, :] = kp_ref[...]
    kext[A_HALF:A_HALF + tq, :] = k_ref[...]
    kext[A_HALF + tq:, :] = kn_ref[...]
    for hd in range(A_HEADS):
        src = slice(hd * A_HEAD_DIM, (hd + 1) * A_HEAD_DIM)
        dst = slice(2 * hd * A_HEAD_DIM, (2 * hd + 1) * A_HEAD_DIM)
        vext[0:A_HALF, dst] = vp_ref[:, src]
        vext[A_HALF:A_HALF + tq, dst] = v_ref[:, src]
        vext[A_HALF + tq:, dst] = vn_ref[:, src]

    qi = lax.broadcasted_iota(jnp.int32, (A_QBLK, A_KBLK), 0)
    ci = lax.broadcasted_iota(jnp.int32, (A_QBLK, A_KBLK), 1)
    lane = lax.broadcasted_iota(jnp.int32, (A_QBLK, LANES), 1)
    rel = ci - qi
    for j in range(tq // A_QBLK):
        key = l0 + (j * A_QBLK - A_HALF) + ci
        bias = jnp.where(rel >= 0, jnp.where(rel <= 2 * A_HALF, 0.0, NEG), NEG)
        bias = jnp.where(key >= 0, jnp.where(key < rows, bias, NEG), NEG)
        r0 = j * A_QBLK
        lse_tile = jnp.zeros((A_QBLK, LANES), F32)
        for hd in range(A_HEADS):
            c0 = hd * A_HEAD_DIM
            q = q_ref[r0:r0 + A_QBLK, c0:c0 + A_HEAD_DIM]
            k = kext[r0:r0 + A_KBLK, c0:c0 + A_HEAD_DIM]
            v1 = vext[r0:r0 + A_KBLK, 2 * c0:2 * c0 + 2 * A_HEAD_DIM]
            s = _dot_nt(q, k) + bias
            m = jnp.max(s, axis=1, keepdims=True)
            p = jnp.exp2((s - m).astype(BF16))
            acc = _dot(p, v1)
            den = acc[:, A_HEAD_DIM:]
            o_ref[r0:r0 + A_QBLK, c0:c0 + A_HEAD_DIM] = (acc[:, :A_HEAD_DIM] / den).astype(o_ref.dtype)
            lse_tile = jnp.where(lane == hd, m * LN2 + jnp.log(den), lse_tile)
        lse_ref[r0:r0 + A_QBLK, :] = lse_tile


def _attn_group(qkv, *, tq=1024):
    bsz, dil, rows, _ = qkv.shape
    tq = min(tq, rows)
    nt = rows // tq
    nh = tq // A_HALF
    last = rows // A_HALF - 1

    def main(col, width=A_WIDTH):
        return pl.BlockSpec((None, None, tq, width), lambda b, n: (b, n // nt, n % nt, col))

    def halo_prev(col):
        return pl.BlockSpec((None, None, A_HALF, A_WIDTH),
                            lambda b, n: (b, n // nt, jnp.maximum((n % nt) * nh - 1, 0), col))

    def halo_next(col):
        return pl.BlockSpec((None, None, A_HALF, A_WIDTH),
                            lambda b, n: (b, n // nt, jnp.minimum((n % nt + 1) * nh, last), col))

    return pl.pallas_call(
        functools.partial(_attn_kernel, tq=tq, rows=rows),
        grid=(bsz, dil * nt),
        in_specs=[main(0), main(1), main(2), halo_prev(1), halo_next(1), halo_prev(2), halo_next(2)],
        out_specs=[main(0), main(0, LANES)],
        out_shape=[jax.ShapeDtypeStruct((bsz, dil, rows, A_WIDTH), BF16),
                   jax.ShapeDtypeStruct((bsz, dil, rows, LANES), F32)],
        scratch_shapes=[pltpu.VMEM((tq + 2 * A_HALF, A_WIDTH), BF16),
                        pltpu.VMEM((tq + 2 * A_HALF, 2 * A_WIDTH), BF16)],
        compiler_params=_params(("arbitrary", "arbitrary")),
        name="attn_group",
    )(qkv, qkv, qkv, qkv, qkv, qkv, qkv)


def _a_out_kernel(*refs, final, dils, tm):
    n_g = len(dils)
    x_ref = refs[0]
    o_refs = refs[1:1 + n_g]
    l_refs = refs[1 + n_g:1 + 2 * n_g]
    n_perm = sum(1 for d in dils if d > 1)
    p_refs = refs[1 + 2 * n_g:1 + 2 * n_g + n_perm]
    g_ref, wg_ref, wo_ref = refs[1 + 2 * n_g + n_perm:4 + 2 * n_g + n_perm]
    fg_ref = refs[-2] if final else None
    y_ref = refs[-1]

    for sub in range(tm // A_PERM):
        rows = slice(sub * A_PERM, (sub + 1) * A_PERM)

        def block(ref, dil):
            n = A_PERM // dil
            return jnp.concatenate([ref[r, sub * n:(sub + 1) * n, :] for r in range(dil)], axis=0)

        outs, lses = [], []
        pi = 0
        for gi, dil in enumerate(dils):
            o = block(o_refs[gi], dil)
            lse = block(l_refs[gi], dil)
            if dil > 1:
                pt = p_refs[pi][...]
                pi += 1
                o = _dot(pt, o)
                both = _dot(pt, jnp.concatenate(_split2(lse), axis=1))
                lse = both[:, :LANES] + both[:, LANES:]
            else:
                o = o.astype(F32)
            outs.append(o)
            lses.append(lse)

        mx = functools.reduce(jnp.maximum, lses)
        ws = [jnp.exp(l - mx) for l in lses]
        inv = 1.0 / functools.reduce(lambda a, b: a + b, ws)
        ws = [w * inv for w in ws]

        x = x_ref[rows, :]
        h = _rmsnorm(x, g_ref[...]).astype(BF16)
        gate = _silu(_dot(h, wg_ref[...]))
        cols = []
        for hd in range(A_HEADS):
            c0 = hd * A_HEAD_DIM
            o = outs[0][:, c0:c0 + A_HEAD_DIM] * ws[0][:, hd:hd + 1]
            for gi in range(1, n_g):
                o = o + outs[gi][:, c0:c0 + A_HEAD_DIM] * ws[gi][:, hd:hd + 1]
            cols.append((o * gate[:, c0:c0 + A_HEAD_DIM]).astype(BF16))
        y = x + _dot(jnp.concatenate(cols, axis=1), wo_ref[...])
        if final:
            y = _rmsnorm(y, fg_ref[...])
        y_ref[rows, :] = y


def _a_out(x, outs, lses, g, w_in, layer, gate_col, w_out, final_g, *, tm=1024):
    bsz, seq, _ = x.shape
    tm = min(tm, seq)
    final = final_g is not None
    dils = tuple(o.shape[1] for o in outs)
    tile = pl.BlockSpec((None, tm, D_MODEL), lambda b, i: (b, i, 0))
    vec = pl.BlockSpec((1, D_MODEL), lambda b, i: (0, 0))
    mat = pl.BlockSpec((D_MODEL, D_MODEL), lambda b, i: (0, 0))
    in_specs = [tile]
    args = [x]
    for width, arrs in ((A_WIDTH, outs), (LANES, lses)):
        for d, a in zip(dils, arrs):
            in_specs.append(pl.BlockSpec((None, d, tm // d, width), lambda b, i: (b, 0, i, 0)))
            args.append(a)
    for d in dils:
        if d > 1:
            in_specs.append(pl.BlockSpec((A_PERM, A_PERM), lambda b, i: (0, 0)))
            args.append(jnp.asarray(_deinterleave_matrix(d).T, BF16))
    in_specs += [vec, pl.BlockSpec((None, D_MODEL, D_MODEL), lambda b, i: (layer, 0, gate_col)), mat]
    args += [g.reshape(1, D_MODEL), w_in, w_out]
    if final:
        in_specs.append(vec)
        args.append(final_g.reshape(1, D_MODEL))
    return pl.pallas_call(
        functools.partial(_a_out_kernel, final=final, dils=dils, tm=tm),
        grid=(bsz, seq // tm),
        in_specs=in_specs,
        out_specs=tile,
        out_shape=jax.ShapeDtypeStruct((bsz, seq, D_MODEL), F32),
        compiler_params=_params(("arbitrary", "arbitrary")),
        name="a_out",
    )(*args)


def _rope_base(seq):
    half = ROPE_DIM // 2
    pos = jnp.arange(seq, dtype=F32)
    inv = ROPE_THETA ** (-(jnp.arange(half, dtype=F32) * 2.0) / ROPE_DIM)
    ang = inv[:, None] * pos[None, :]
    cs = jnp.concatenate([jnp.cos(ang), jnp.sin(ang)], axis=0).T
    return jnp.concatenate([cs, jnp.zeros((seq, LANES - ROPE_DIM), F32)], axis=1)


def _rope_tables(base, seq, dil):
    t = base[:seq]
    if dil > 1:
        t = t.reshape(seq // A_PERM, A_PERM // dil, dil, LANES)
        t = jnp.swapaxes(t, 1, 2).reshape(seq, LANES)
    return t


def _mixer_a(x, g, w_in, layer, w_out, final_g, rope_base):
    seq = x.shape[1]
    n_grp = len(A_GROUPS)
    outs, lses = [], []
    for grp, (_, dil) in enumerate(A_GROUPS):
        qkv = _norm_proj(x, g, w_in, layer, [grp, n_grp + grp, 2 * n_grp + grp], dil=dil,
                         rope=_rope_tables(rope_base, seq, dil), n_rope=2, q_scale=A_HEAD_DIM ** -0.5 / LN2)
        o, lse = _attn_group(qkv)
        outs.append(o)
        lses.append(lse)
    return _a_out(x, outs, lses, g, w_in, layer, 3 * n_grp, w_out.astype(BF16), final_g)


def _rglru_kernel(*refs, tile, n_tiles, reverse, n_next):
    step = pl.program_id(1)
    if reverse:
        (xc_ref, wg_ref, br_ref, bi_ref, lam_ref,
         hf_ref, gate_ref, x_ref, wo_ref, gnext_ref, wnext_ref, out_ref, pnext_ref,
         a_scr, u_scr, h_scr) = refs
        xc = xc_ref[...]
    else:
        (xm_ref, xp_ref, xn_ref, cw_ref, cb_ref, wg_ref, br_ref, bi_ref, lam_ref,
         out_ref, xc_ref, a_scr, u_scr, h_scr) = refs
        prev = jnp.where(step > 0, xp_ref[...].astype(F32), 0.0)
        nxt = jnp.where(step < n_tiles - 1, xn_ref[...].astype(F32), 0.0)
        xm = xm_ref[...].astype(F32)
        ext = jnp.concatenate([prev, xm, nxt], axis=0)
        n_ext = tile + 2 * BF16_ROWS
        xm1 = pltpu.roll(ext, 1, 0)[BF16_ROWS:BF16_ROWS + tile, :]
        xm2 = pltpu.roll(ext, 2, 0)[BF16_ROWS:BF16_ROWS + tile, :]
        xp1 = pltpu.roll(ext, n_ext - 1, 0)[BF16_ROWS:BF16_ROWS + tile, :]
        cw = cw_ref[...]
        xc = xm2 * cw[0:1, :] + xm1 * cw[1:2, :] + xm * cw[2:3, :] + xp1 * cw[3:4, :] + cb_ref[...]
        xc_ref[...] = xc

    lam = lam_ref[...]
    softplus = jnp.log1p(jnp.exp(-jnp.abs(lam))) + jnp.maximum(-lam, 0.0)
    for n in range(B_BLOCKS):
        c0 = n * B_BLOCK_DIM
        xcn = xc[:, c0:c0 + B_BLOCK_DIM]
        gates = _dot(xcn.astype(BF16), wg_ref[n])
        r = _sigmoid(gates[:, :B_BLOCK_DIM] + br_ref[:, c0:c0 + B_BLOCK_DIM])
        ig = _sigmoid(gates[:, B_BLOCK_DIM:] + bi_ref[:, c0:c0 + B_BLOCK_DIM])
        log_a = -B_C * r * softplus[:, c0:c0 + B_BLOCK_DIM]
        a = jnp.exp(log_a)
        mult = jnp.sqrt(-jnp.tanh(log_a) * (a * a + 1.0))
        a_scr[:, c0:c0 + B_BLOCK_DIM] = a
        u_scr[:, c0:c0 + B_BLOCK_DIM] = mult * (ig * xcn)

    @pl.when(step == 0)
    def _():
        h_scr[...] = jnp.zeros_like(h_scr)

    n_groups = tile // SUBLANES
    row8 = lax.broadcasted_iota(jnp.int32, (SUBLANES, D_MODEL), 0)

    def body(gi, h):
        g = (n_groups - 1 - gi) if reverse else gi
        off = pl.multiple_of(g * SUBLANES, SUBLANES)
        a8 = a_scr[pl.ds(off, SUBLANES), :]
        u8 = u_scr[pl.ds(off, SUBLANES), :]
        for s in (1, 2, 4):
            if reverse:
                a_sh = pltpu.roll(a8, SUBLANES - s, 0)
                u_sh = pltpu.roll(u8, SUBLANES - s, 0)
                keep = row8 < SUBLANES - s
            else:
                a_sh = pltpu.roll(a8, s, 0)
                u_sh = pltpu.roll(u8, s, 0)
                keep = row8 >= s
            u8 = jnp.where(keep, a8 * u_sh + u8, u8)
            a8 = jnp.where(keep, a8 * a_sh, a8)
        hs = a8 * h + u8
        u_scr[pl.ds(off, SUBLANES), :] = hs
        return hs[0:1, :] if reverse else hs[SUBLANES - 1:SUBLANES, :]

    h_scr[...] = lax.fori_loop(0, n_groups, body, h_scr[...], unroll=4)

    if reverse:
        y = hf_ref[...] + u_scr[...]
        z = (y * _silu(gate_ref[...].astype(F32))).astype(BF16)
        x_new = x_ref[...] + _dot(z, wo_ref[...])
        out_ref[...] = x_new
        h_next = _rmsnorm(x_new, gnext_ref[...]).astype(BF16)
        for c in range(n_next):
            cols = slice(c * D_MODEL, (c + 1) * D_MODEL)
            pnext_ref[:, cols] = _dot(h_next, wnext_ref[:, cols]).astype(pnext_ref.dtype)
    else:
        out_ref[...] = u_scr[...]


def _rglru(proj, conv_w, conv_b, wg, b_r, b_i, lam, *, reverse, xc=None, hf=None, x=None, w_out=None,
           g_next=None, w_next=None, tile=512):
    bsz, seq, _ = proj.shape
    tile = min(tile, seq)
    n_tiles = seq // tile
    per = tile // BF16_ROWS
    last = seq // BF16_ROWS - 1

    def tix(i):
        return (n_tiles - 1 - i) if reverse else i

    main = pl.BlockSpec((None, tile, D_MODEL), lambda b, i: (b, tix(i), 0))
    vec = pl.BlockSpec((1, D_MODEL), lambda b, i: (0, 0))
    gate_w = pl.BlockSpec((B_BLOCKS, B_BLOCK_DIM, 2 * B_BLOCK_DIM), lambda b, i: (0, 0, 0))
    gate_args = [wg, b_r.reshape(1, D_MODEL), b_i.reshape(1, D_MODEL), lam.reshape(1, D_MODEL)]
    f32_out = jax.ShapeDtypeStruct((bsz, seq, D_MODEL), F32)
    n_next = 0
    if reverse:
        layer_next, width_next = w_next[1], w_next[0].shape[2]
        n_next = width_next // D_MODEL
        once = pl.Buffered(1)
        in_specs = [main, gate_w, vec, vec, vec,
                    main, pl.BlockSpec((None, tile, D_MODEL), lambda b, i: (b, tix(i), 1)), main,
                    pl.BlockSpec((D_MODEL, D_MODEL), lambda b, i: (0, 0), pipeline_mode=once), vec,
                    pl.BlockSpec((None, D_MODEL, width_next), lambda b, i: (layer_next, 0, 0),
                                 pipeline_mode=once)]
        args = [xc] + gate_args + [hf, proj, x, w_out, g_next.reshape(1, D_MODEL), w_next[0]]
        out_specs = [main, pl.BlockSpec((None, tile, width_next), lambda b, i: (b, tix(i), 0))]
        out_shape = [f32_out, jax.ShapeDtypeStruct((bsz, seq, width_next), BF16)]
    else:
        halo_p = pl.BlockSpec((None, BF16_ROWS, D_MODEL), lambda b, i: (b, jnp.maximum(i * per - 1, 0), 0))
        halo_n = pl.BlockSpec((None, BF16_ROWS, D_MODEL), lambda b, i: (b, jnp.minimum((i + 1) * per, last), 0))
        in_specs = [main, halo_p, halo_n, pl.BlockSpec((4, D_MODEL), lambda b, i: (0, 0)), vec,
                    gate_w, vec, vec, vec]
        args = [proj, proj, proj, conv_w, conv_b.reshape(1, D_MODEL)] + gate_args
        out_specs = [main, main]
        out_shape = [f32_out, f32_out]
    return pl.pallas_call(
        functools.partial(_rglru_kernel, tile=tile, n_tiles=n_tiles, reverse=reverse, n_next=n_next),
        grid=(bsz, n_tiles),
        in_specs=in_specs,
        out_specs=out_specs,
        out_shape=out_shape,
        scratch_shapes=[pltpu.VMEM((tile, D_MODEL), F32), pltpu.VMEM((tile, D_MODEL), F32),
                        pltpu.VMEM((1, D_MODEL), F32)],
        compiler_params=_params(("arbitrary", "arbitrary")),
        name="rglru_bwd_out" if reverse else "rglru_fwd",
    )(*args)


def _mixer_b(x, g, w_in, layer, conv_w, conv_b, w_r, b_r, w_i, b_i, lam, w_out, g_next, w_next):
    bsz, seq, _ = x.shape
    proj = _norm_proj(x, g, w_in, layer, [0, 1]).reshape(bsz, seq, 2 * D_MODEL)
    wg = [jnp.concatenate([w_r[d], w_i[d]], axis=-1).astype(BF16) for d in range(2)]
    hf, xc = _rglru(proj, conv_w, conv_b, wg[0], b_r[0], b_i[0], lam[0], reverse=False)
    return _rglru(proj, conv_w, conv_b, wg[1], b_r[1], b_i[1], lam[1], reverse=True,
                  xc=xc, hf=hf, x=x, w_out=w_out.astype(BF16), g_next=g_next, w_next=(w_next, layer))


def _split3(x):
    hi = x.astype(BF16)
    r1 = x - hi.astype(F32)
    mid = r1.astype(BF16)
    lo = (r1 - mid.astype(F32)).astype(BF16)
    return hi, mid, lo


def _hgrn_kernel(*refs, tile, reverse):
    if reverse:
        (q_ref, z_ref, v_ref, lb_ref, tri_ref, of_ref, gate_ref, x_ref, gn_ref, wo_ref,
         out_ref, st_scr, o_scr) = refs
    else:
        q_ref, z_ref, v_ref, lb_ref, tri_ref, out_ref, st_scr, o_scr = refs

    @pl.when(pl.program_id(1) == 0)
    def _():
        st_scr[...] = jnp.zeros_like(st_scr)

    n_blk = C_TILE // C_CHUNK
    order = list(reversed(range(n_blk))) if reverse else list(range(n_blk))
    mid_row = C_CHUNK // 2 - 1 if reverse else C_CHUNK // 2
    end_row = 0 if reverse else C_CHUNK - 1
    lb = lb_ref[...]
    one_m_lb = 1.0 - lb
    tri = tri_ref[...]

    n_sub = tile // C_TILE
    for sub in (reversed(range(n_sub)) if reverse else range(n_sub)):
        base = sub * C_TILE
        qt, kt, rho, blk_total = {}, {}, {}, {}
        for j in order:
            r0 = base + j * C_CHUNK
            z = z_ref[r0:r0 + C_CHUNK, :].astype(F32)
            half_t = 0.5 * jnp.tanh(0.5 * z)
            sig_pos = 0.5 + half_t
            sig_neg = 0.5 - half_t
            logf = jnp.log(lb + one_m_lb * sig_pos)
            b = _dot(tri, jnp.concatenate(_split3(logf), axis=0))
            b_mid = b[mid_row:mid_row + 1, :]
            qt[j] = q_ref[r0:r0 + C_CHUNK, :].astype(F32) * jnp.exp(b - b_mid)
            kt[j] = (one_m_lb * sig_neg) * jnp.exp(b_mid - b)
            rho[j] = b_mid
            blk_total[j] = b[end_row:end_row + 1, :]

        off = jnp.zeros_like(lb)
        for j in order:
            rho[j] = rho[j] + off
            off = off + blk_total[j]
        b_end = off
        decay = jnp.exp(b_end)

        qd = jnp.concatenate([(qt[j] * jnp.exp(rho[j])).astype(BF16) for j in range(n_blk)], axis=0)
        kd = jnp.concatenate([(kt[j] * jnp.exp(b_end - rho[j])).astype(BF16) for j in range(n_blk)], axis=0)

        zeros = jnp.zeros((C_CHUNK, D_MODEL), BF16)
        keys = {}
        for i in range(n_blk):
            parts = []
            for j in range(n_blk):
                if j == i:
                    parts.append(kt[j].astype(BF16))
                elif order.index(j) < order.index(i):
                    parts.append((kt[j] * jnp.exp(rho[i] - rho[j])).astype(BF16))
                else:
                    parts.append(zeros)
            keys[i] = jnp.concatenate(parts, axis=0)
        qt = {j: qt[j].astype(BF16) for j in qt}

        ti = lax.broadcasted_iota(jnp.int32, (C_TILE, C_TILE), 0)
        si = lax.broadcasted_iota(jnp.int32, (C_TILE, C_TILE), 1)
        visible = (si >= ti) if reverse else (si <= ti)
        zq = jnp.zeros((C_CHUNK, C_KDIM), BF16)
        rows = slice(base, base + C_TILE)
        for hd in range(C_HEADS):
            cs = slice(hd * C_KDIM, (hd + 1) * C_KDIM)
            q_diag = jnp.concatenate(
                [jnp.concatenate([qt[i][:, cs] if j == i else zq for j in range(n_blk)], axis=1)
                 for i in range(n_blk)], axis=0)
            k_cat = jnp.concatenate([keys[i][:, cs] for i in range(n_blk)], axis=1)
            att = jnp.where(visible, _dot_nt(q_diag, k_cat), 0.0).astype(BF16)
            st = st_scr[hd]
            o_scr[rows, cs] = _dot(att, v_ref[rows, cs]) + _dot_nt(qd[:, cs], st.astype(BF16))
            st_scr[hd] = decay[:, cs] * st + _dot_tn(v_ref[rows, cs], kd[:, cs])

    if reverse:
        gn = gn_ref[...]
        for hd in range(C_HEADS):
            c0 = hd * C_KDIM
            o = of_ref[:, c0:c0 + C_KDIM] + o_scr[:, c0:c0 + C_KDIM]
            o = o * lax.rsqrt(jnp.mean(o * o, axis=-1, keepdims=True) + EPS) * gn
            o_scr[:, c0:c0 + C_KDIM] = o
        z = (o_scr[...] * _silu(gate_ref[...].astype(F32))).astype(BF16)
        out_ref[...] = x_ref[...] + _dot(z, wo_ref[...])
    else:
        out_ref[...] = o_scr[...]


def _hgrn(proj, lb, *, reverse, of=None, x=None, gn=None, w_out=None, tile=512):
    bsz, seq, _ = proj.shape
    tile = min(tile, seq)
    n_tiles = seq // tile

    def tix(i):
        return (n_tiles - 1 - i) if reverse else i

    def col(j):
        return pl.BlockSpec((None, tile, D_MODEL), lambda b, i: (b, tix(i), j))

    ti = lax.broadcasted_iota(jnp.int32, (C_CHUNK, C_CHUNK), 0)
    si = lax.broadcasted_iota(jnp.int32, (C_CHUNK, C_CHUNK), 1)
    tri = ((si >= ti) if reverse else (si <= ti)).astype(BF16)
    tri = jnp.concatenate([tri, tri, tri], axis=1)
    in_specs = [col(0), col(2 if reverse else 1), col(3),
                pl.BlockSpec((1, D_MODEL), lambda b, i: (0, 0)),
                pl.BlockSpec((C_CHUNK, 3 * C_CHUNK), lambda b, i: (0, 0))]
    args = [proj, proj, proj, lb.reshape(1, D_MODEL), tri]
    if reverse:
        in_specs += [col(0), col(4), col(0),
                     pl.BlockSpec((1, C_KDIM), lambda b, i: (0, 0)),
                     pl.BlockSpec((D_MODEL, D_MODEL), lambda b, i: (0, 0))]
        args += [of, proj, x, gn.reshape(1, C_KDIM), w_out]
    return pl.pallas_call(
        functools.partial(_hgrn_kernel, tile=tile, reverse=reverse),
        grid=(bsz, n_tiles),
        in_specs=in_specs,
        out_specs=col(0),
        out_shape=jax.ShapeDtypeStruct((bsz, seq, D_MODEL), F32),
        scratch_shapes=[pltpu.VMEM((C_HEADS, C_KDIM, C_KDIM), F32), pltpu.VMEM((tile, D_MODEL), F32)],
        compiler_params=_params(("arbitrary", "arbitrary")),
        name="hgrn_bwd_out" if reverse else "hgrn_fwd",
    )(*args)


def _mixer_c(x, proj, lb, gn, w_out):
    of = _hgrn(proj, lb[0], reverse=False)
    return _hgrn(proj, lb[1], reverse=True, of=of, x=x, gn=gn, w_out=w_out.astype(BF16))


def _trunk(x, rope_base, norm_g, final_g, a_w_in, a_w_out, b_w_in, b_conv_w, b_conv_b, b_w_r, b_b_r, b_w_i, b_b_i,
           b_lambda, b_w_out, c_w_in, c_lower_bounds, c_gnorm_g, c_w_out):
    depth = norm_g.shape[0]
    sm = jax.nn.softmax(c_lower_bounds.astype(F32), axis=0)
    lbs = jnp.cumsum(sm, axis=0) - sm[0]
    if depth % N_MIXERS != 1 or depth < 1:
        raise NotImplementedError("trunk depth must end on an attention layer")
    proj_c = None
    for layer in range(depth):
        kind = layer % N_MIXERS
        j = layer // N_MIXERS
        if kind == 0:
            x = _mixer_a(x, norm_g[layer], a_w_in, j, a_w_out[j], final_g if layer == depth - 1 else None,
                         rope_base)
        elif kind == 1:
            x, proj_c = _mixer_b(x, norm_g[layer], b_w_in, j, b_conv_w[j], b_conv_b[j], b_w_r[j], b_b_r[j],
                                 b_w_i[j], b_b_i[j], b_lambda[j], b_w_out[j], norm_g[layer + 1], c_w_in)
        else:
            x = _mixer_c(x, proj_c, lbs[layer], c_gnorm_g[j], c_w_out[j])
    return x


def kernel(x_prompt, x_sample, norm_g, final_g, a_w_in, a_w_out, b_w_in, b_conv_w, b_conv_b, b_w_r, b_b_r,
           b_w_i, b_b_i, b_lambda, b_w_out, c_w_in, c_lower_bounds, c_gnorm_g, c_w_out):
    weights = (norm_g, final_g, a_w_in.astype(BF16), a_w_out, b_w_in.astype(BF16), b_conv_w, b_conv_b,
               b_w_r, b_b_r, b_w_i, b_b_i, b_lambda, b_w_out, c_w_in.astype(BF16), c_lower_bounds,
               c_gnorm_g, c_w_out)
    rope_base = _rope_base(max(x_prompt.shape[1], x_sample.shape[1]))
    return (_trunk(x_prompt, rope_base, *weights), _trunk(x_sample, rope_base, *weights))
```

```python
import functools

import numpy as np
import jax
import jax.numpy as jnp
from jax import lax
from jax.experimental import pallas as pl
from jax.experimental.pallas import tpu as pltpu

F32 = jnp.float32
BF16 = jnp.bfloat16

D_MODEL = 1024
EPS = 1e-6
LANES = 128
SUBLANES = 8
BF16_ROWS = 16
VMEM_LIMIT = 56 * 1024 * 1024

N_MIXERS = 3
A_GROUPS = ((128, 1), (512, 4), (2048, 16))
A_HEADS = 8
A_HEAD_DIM = 128
A_WIDTH = A_HEADS * A_HEAD_DIM
A_HALF = 64
ROPE_THETA = 500000.0
ROPE_DIM = A_HEAD_DIM // 4
A_QBLK = 128
A_KBLK = A_QBLK + 2 * A_HALF
A_PERM = 256
PROJ_PIECE = 256
NEG = -1e30
LN2 = 0.6931471805599453
B_BLOCKS = 8
B_BLOCK_DIM = D_MODEL // B_BLOCKS
B_C = 8.0
C_HEADS = 8
C_KDIM = 128
C_CHUNK = 64
C_TILE = 256


def _params(semantics):
    return pltpu.CompilerParams(dimension_semantics=semantics, vmem_limit_bytes=VMEM_LIMIT)


def _rmsnorm(x, g):
    ms = jnp.mean(x * x, axis=-1, keepdims=True)
    return x * lax.rsqrt(ms + EPS) * g


def _sigmoid(x):
    return 0.5 * jnp.tanh(0.5 * x) + 0.5


def _silu(g):
    return g * _sigmoid(g)


def _dot(a, b):
    return jnp.dot(a, b, preferred_element_type=F32)


def _dot_nt(a, b):
    return lax.dot_general(a, b, (((1,), (1,)), ((), ())), preferred_element_type=F32)


def _dot_tn(a, b):
    return lax.dot_general(a, b, (((0,), (0,)), ((), ())), preferred_element_type=F32)


def _split2(x):
    hi = x.astype(BF16)
    return hi, (x - hi.astype(F32)).astype(BF16)


def _deinterleave_matrix(dil):
    n = A_PERM // dil
    idx = np.arange(A_PERM)
    p = np.zeros((A_PERM, A_PERM), np.float32)
    p[idx, (idx % n) * dil + idx // n] = 1.0
    return p


def _norm_proj_kernel(*refs, n_chunks, n_rope, q_scale, dil, tl):
    in_refs, o_ref = refs[:-1], refs[-1]
    x_ref, g_ref = in_refs[:2]
    w_refs = in_refs[2:2 + n_chunks]
    p_ref = in_refs[2 + n_chunks] if dil > 1 else None
    if n_rope:
        half = ROPE_DIM // 2
        t = in_refs[-1][...]
        lane = lax.broadcasted_iota(jnp.int32, t.shape, 1)
        lo_half, hi_half = lane < half, lane < ROPE_DIM
        ta = jnp.where(lo_half, t, jnp.where(hi_half, pltpu.roll(t, half, 1), 1.0))
        tb = jnp.where(lo_half, -pltpu.roll(t, LANES - half, 1), 0.0)
        tc = jnp.where(lo_half, 0.0, jnp.where(hi_half, t, 0.0))
    n_sub = tl // A_PERM
    n = A_PERM // dil

    h = _rmsnorm(x_ref[...], g_ref[...]).astype(BF16)
    if dil > 1:
        p = p_ref[...]
        h = jnp.concatenate(
            [_dot(p, h[s * A_PERM:(s + 1) * A_PERM, :]).astype(BF16) for s in range(n_sub)], axis=0)

    def store(val, lo, width):
        val = val.astype(o_ref.dtype)
        if dil == 1:
            o_ref[0, :, lo:lo + width] = val
        else:
            for s in range(n_sub):
                for r in range(dil):
                    a = s * A_PERM + r * n
                    o_ref[r, s * n:(s + 1) * n, lo:lo + width] = val[a:a + n, :]

    for lo in range(0, n_chunks * D_MODEL, PROJ_PIECE):
        acc = _dot(h, w_refs[lo // D_MODEL][:, lo % D_MODEL:lo % D_MODEL + PROJ_PIECE])
        if lo < n_rope * D_MODEL:
            for hd in range(PROJ_PIECE // LANES):
                s = acc[:, hd * LANES:(hd + 1) * LANES]
                r = (s * ta + pltpu.roll(s, LANES - ROPE_DIM // 2, 1) * tb
                     + pltpu.roll(s, ROPE_DIM // 2, 1) * tc)
                if lo < D_MODEL and q_scale is not None:
                    r = r * q_scale
                store(r, lo + hd * LANES, LANES)
        else:
            store(acc, lo, PROJ_PIECE)


def _norm_proj(x, g, w, layer, cols, *, dil=1, rope=None, n_rope=0, q_scale=None, tl=1024):
    bsz, seq, _ = x.shape
    n = len(cols) * D_MODEL
    tl = min(tl, seq)
    full = lambda b, i: (0, 0)
    in_specs = [
        pl.BlockSpec((None, tl, D_MODEL), lambda b, i: (b, i, 0)),
        pl.BlockSpec((1, D_MODEL), full),
    ]
    args = [x, g.reshape(1, D_MODEL)]
    for col in cols:
        in_specs.append(pl.BlockSpec((None, D_MODEL, D_MODEL), lambda b, i, col=col: (layer, 0, col)))
        args.append(w)
    if dil > 1:
        in_specs.append(pl.BlockSpec((A_PERM, A_PERM), full))
        args.append(jnp.asarray(_deinterleave_matrix(dil), BF16))
    if n_rope:
        in_specs.append(pl.BlockSpec((tl, LANES), lambda b, i: (i, 0)))
        args.append(rope)
    return pl.pallas_call(
        functools.partial(_norm_proj_kernel, n_chunks=n // D_MODEL, n_rope=n_rope, q_scale=q_scale,
                          dil=dil, tl=tl),
        grid=(bsz, seq // tl),
        in_specs=in_specs,
        out_specs=pl.BlockSpec((None, dil, tl // dil, n), lambda b, i: (b, 0, i, 0)),
        out_shape=jax.ShapeDtypeStruct((bsz, dil, seq // dil, n), BF16),
        compiler_params=_params(("arbitrary", "arbitrary")),
        name="norm_proj",
    )(*args)


def _attn_kernel(q_ref, k_ref, v_ref, kp_ref, kn_ref, vp_ref, vn_ref, o_ref, lse_ref, kext, vext, *, tq, rows):
    step = pl.program_id(1)
    l0 = (step % (rows // tq)) * tq
    first = (pl.program_id(0) == 0) & (step == 0)

    @pl.when(first)
    def _():
        vext[...] = jnp.ones_like(vext)

    edge = min(A_KBLK - A_HALF, tq)
    kext[0:A_HALF, :] = kp_ref[...]
    kext[A_HALF:A_HALF + edge, :] = k_ref[0:edge, :]
    kext[A_HALF + tq - edge:A_HALF + tq, :] = k_ref[tq - edge:tq, :]
    kext[A_HALF + tq:, :] = kn_ref[...]
    for hd in range(A_HEADS):
        src = slice(hd * A_HEAD_DIM, (hd + 1) * A_HEAD_DIM)
        dst = slice(2 * hd * A_HEAD_DIM, (2 * hd + 1) * A_HEAD_DIM)
        vext[0:A_HALF, dst] = vp_ref[:, src]
        vext[A_HALF:A_HALF + tq, dst] = v_ref[:, src]
        vext[A_HALF + tq:, dst] = vn_ref[:, src]

    qi = lax.broadcasted_iota(jnp.int32, (A_QBLK, A_KBLK), 0)
    ci = lax.broadcasted_iota(jnp.int32, (A_QBLK, A_KBLK), 1)
    lane = lax.broadcasted_iota(jnp.int32, (A_QBLK, LANES), 1)
    rel = ci - qi
    for j in range(tq // A_QBLK):
        key = l0 + (j * A_QBLK - A_HALF) + ci
        bias = jnp.where(rel >= 0, jnp.where(rel <= 2 * A_HALF, 0.0, NEG), NEG)
        bias = jnp.where(key >= 0, jnp.where(key < rows, bias, NEG), NEG)
        r0 = j * A_QBLK
        lse_tile = jnp.zeros((A_QBLK, LANES), F32)
        for hd in range(A_HEADS):
            c0 = hd * A_HEAD_DIM
            q = q_ref[r0:r0 + A_QBLK, c0:c0 + A_HEAD_DIM]
            if 0 < j < tq // A_QBLK - 1:
                k = k_ref[r0 - A_HALF:r0 - A_HALF + A_KBLK, c0:c0 + A_HEAD_DIM]
            else:
                k = kext[r0:r0 + A_KBLK, c0:c0 + A_HEAD_DIM]
            v1 = vext[r0:r0 + A_KBLK, 2 * c0:2 * c0 + 2 * A_HEAD_DIM]
            s = _dot_nt(q, k) + bias
            m = jnp.max(s, axis=1, keepdims=True)
            p = jnp.exp2((s - m).astype(BF16))
            acc = _dot(p, v1)
            den = acc[:, A_HEAD_DIM:]
            o_ref[r0:r0 + A_QBLK, c0:c0 + A_HEAD_DIM] = (acc[:, :A_HEAD_DIM] / den).astype(o_ref.dtype)
            lse_tile = jnp.where(lane == hd, m * LN2 + jnp.log(den), lse_tile)
        lse_ref[r0:r0 + A_QBLK, :] = lse_tile


def _attn_group(qkv, *, tq=1024):
    bsz, dil, rows, _ = qkv.shape
    tq = min(tq, rows)
    nt = rows // tq
    nh = tq // A_HALF
    last = rows // A_HALF - 1

    def main(col, width=A_WIDTH):
        return pl.BlockSpec((None, None, tq, width), lambda b, n: (b, n // nt, n % nt, col))

    def halo_prev(col):
        return pl.BlockSpec((None, None, A_HALF, A_WIDTH),
                            lambda b, n: (b, n // nt, jnp.maximum((n % nt) * nh - 1, 0), col))

    def halo_next(col):
        return pl.BlockSpec((None, None, A_HALF, A_WIDTH),
                            lambda b, n: (b, n // nt, jnp.minimum((n % nt + 1) * nh, last), col))

    return pl.pallas_call(
        functools.partial(_attn_kernel, tq=tq, rows=rows),
        grid=(bsz, dil * nt),
        in_specs=[main(0), main(1), main(2), halo_prev(1), halo_next(1), halo_prev(2), halo_next(2)],
        out_specs=[main(0), main(0, LANES)],
        out_shape=[jax.ShapeDtypeStruct((bsz, dil, rows, A_WIDTH), BF16),
                   jax.ShapeDtypeStruct((bsz, dil, rows, LANES), F32)],
        scratch_shapes=[pltpu.VMEM((tq + 2 * A_HALF, A_WIDTH), BF16),
                        pltpu.VMEM((tq + 2 * A_HALF, 2 * A_WIDTH), BF16)],
        compiler_params=_params(("arbitrary", "arbitrary")),
        name="attn_group",
    )(qkv, qkv, qkv, qkv, qkv, qkv, qkv)


def _a_out_kernel(*refs, final, dils, tm):
    n_g = len(dils)
    x_ref = refs[0]
    o_refs = refs[1:1 + n_g]
    l_refs = refs[1 + n_g:1 + 2 * n_g]
    n_perm = sum(1 for d in dils if d > 1)
    p_refs = refs[1 + 2 * n_g:1 + 2 * n_g + n_perm]
    g_ref, wg_ref, wo_ref = refs[1 + 2 * n_g + n_perm:4 + 2 * n_g + n_perm]
    fg_ref = refs[-2] if final else None
    y_ref = refs[-1]

    for sub in range(tm // A_PERM):
        rows = slice(sub * A_PERM, (sub + 1) * A_PERM)

        def block(ref, dil):
            n = A_PERM // dil
            return jnp.concatenate([ref[r, sub * n:(sub + 1) * n, :] for r in range(dil)], axis=0)

        outs, lses = [], []
        pi = 0
        for gi, dil in enumerate(dils):
            o = block(o_refs[gi], dil)
            lse = block(l_refs[gi], dil)
            if dil > 1:
                pt = p_refs[pi][...]
                pi += 1
                o = _dot(pt, o)
                both = _dot(pt, jnp.concatenate(_split2(lse), axis=1))
                lse = both[:, :LANES] + both[:, LANES:]
            else:
                o = o.astype(F32)
            outs.append(o)
            lses.append(lse)

        mx = functools.reduce(jnp.maximum, lses)
        ws = [jnp.exp(l - mx) for l in lses]
        inv = 1.0 / functools.reduce(lambda a, b: a + b, ws)
        ws = [w * inv for w in ws]

        x = x_ref[rows, :]
        h = _rmsnorm(x, g_ref[...]).astype(BF16)
        gate = _silu(_dot(h, wg_ref[...]))
        cols = []
        for hd in range(A_HEADS):
            c0 = hd * A_HEAD_DIM
            o = outs[0][:, c0:c0 + A_HEAD_DIM] * ws[0][:, hd:hd + 1]
            for gi in range(1, n_g):
                o = o + outs[gi][:, c0:c0 + A_HEAD_DIM] * ws[gi][:, hd:hd + 1]
            cols.append((o * gate[:, c0:c0 + A_HEAD_DIM]).astype(BF16))
        y = x + _dot(jnp.concatenate(cols, axis=1), wo_ref[...])
        if final:
            y = _rmsnorm(y, fg_ref[...])
        y_ref[rows, :] = y


def _a_out(x, outs, lses, g, w_in, layer, gate_col, w_out, final_g, *, tm=1024):
    bsz, seq, _ = x.shape
    tm = min(tm, seq)
    final = final_g is not None
    dils = tuple(o.shape[1] for o in outs)
    tile = pl.BlockSpec((None, tm, D_MODEL), lambda b, i: (b, i, 0))
    vec = pl.BlockSpec((1, D_MODEL), lambda b, i: (0, 0))
    mat = pl.BlockSpec((D_MODEL, D_MODEL), lambda b, i: (0, 0))
    in_specs = [tile]
    args = [x]
    for width, arrs in ((A_WIDTH, outs), (LANES, lses)):
        for d, a in zip(dils, arrs):
            in_specs.append(pl.BlockSpec((None, d, tm // d, width), lambda b, i: (b, 0, i, 0)))
            args.append(a)
    for d in dils:
        if d > 1:
            in_specs.append(pl.BlockSpec((A_PERM, A_PERM), lambda b, i: (0, 0)))
            args.append(jnp.asarray(_deinterleave_matrix(d).T, BF16))
    in_specs += [vec, pl.BlockSpec((None, D_MODEL, D_MODEL), lambda b, i: (layer, 0, gate_col)), mat]
    args += [g.reshape(1, D_MODEL), w_in, w_out]
    if final:
        in_specs.append(vec)
        args.append(final_g.reshape(1, D_MODEL))
    return pl.pallas_call(
        functools.partial(_a_out_kernel, final=final, dils=dils, tm=tm),
        grid=(bsz, seq // tm),
        in_specs=in_specs,
        out_specs=tile,
        out_shape=jax.ShapeDtypeStruct((bsz, seq, D_MODEL), F32),
        compiler_params=_params(("arbitrary", "arbitrary")),
        name="a_out",
    )(*args)


def _rope_base(seq):
    half = ROPE_DIM // 2
    pos = jnp.arange(seq, dtype=F32)
    inv = ROPE_THETA ** (-(jnp.arange(half, dtype=F32) * 2.0) / ROPE_DIM)
    ang = inv[:, None] * pos[None, :]
    cs = jnp.concatenate([jnp.cos(ang), jnp.sin(ang)], axis=0).T
    return jnp.concatenate([cs, jnp.zeros((seq, LANES - ROPE_DIM), F32)], axis=1)


def _rope_tables(base, seq, dil):
    t = base[:seq]
    if dil > 1:
        t = t.reshape(seq // A_PERM, A_PERM // dil, dil, LANES)
        t = jnp.swapaxes(t, 1, 2).reshape(seq, LANES)
    return t


def _mixer_a(x, g, w_in, layer, w_out, final_g, rope_base):
    seq = x.shape[1]
    n_grp = len(A_GROUPS)
    outs, lses = [], []
    for grp, (_, dil) in enumerate(A_GROUPS):
        qkv = _norm_proj(x, g, w_in, layer, [grp, n_grp + grp, 2 * n_grp + grp], dil=dil,
                         rope=_rope_tables(rope_base, seq, dil), n_rope=2, q_scale=A_HEAD_DIM ** -0.5 / LN2)
        o, lse = _attn_group(qkv)
        outs.append(o)
        lses.append(lse)
    return _a_out(x, outs, lses, g, w_in, layer, 3 * n_grp, w_out.astype(BF16), final_g)


def _rglru_kernel(*refs, tile, n_tiles, reverse, n_next):
    step = pl.program_id(1)
    if reverse:
        (xc_ref, wg_ref, br_ref, bi_ref, lam_ref,
         hf_ref, gate_ref, x_ref, wo_ref, gnext_ref, wnext_ref, out_ref, pnext_ref,
         a_scr, u_scr, h_scr) = refs
        xc = xc_ref[...]
    else:
        (xm_ref, xp_ref, xn_ref, cw_ref, cb_ref, wg_ref, br_ref, bi_ref, lam_ref,
         out_ref, xc_ref, a_scr, u_scr, h_scr) = refs
        prev = jnp.where(step > 0, xp_ref[...].astype(F32), 0.0)
        nxt = jnp.where(step < n_tiles - 1, xn_ref[...].astype(F32), 0.0)
        xm = xm_ref[...].astype(F32)
        ext = jnp.concatenate([prev, xm, nxt], axis=0)
        n_ext = tile + 2 * BF16_ROWS
        xm1 = pltpu.roll(ext, 1, 0)[BF16_ROWS:BF16_ROWS + tile, :]
        xm2 = pltpu.roll(ext, 2, 0)[BF16_ROWS:BF16_ROWS + tile, :]
        xp1 = pltpu.roll(ext, n_ext - 1, 0)[BF16_ROWS:BF16_ROWS + tile, :]
        cw = cw_ref[...]
        xc = xm2 * cw[0:1, :] + xm1 * cw[1:2, :] + xm * cw[2:3, :] + xp1 * cw[3:4, :] + cb_ref[...]
        xc_ref[...] = xc

    lam = lam_ref[...]
    softplus = jnp.log1p(jnp.exp(-jnp.abs(lam))) + jnp.maximum(-lam, 0.0)
    for n in range(B_BLOCKS):
        c0 = n * B_BLOCK_DIM
        xcn = xc[:, c0:c0 + B_BLOCK_DIM]
        gates = _dot(xcn.astype(BF16), wg_ref[n])
        r = _sigmoid(gates[:, :B_BLOCK_DIM] + br_ref[:, c0:c0 + B_BLOCK_DIM])
        ig = _sigmoid(gates[:, B_BLOCK_DIM:] + bi_ref[:, c0:c0 + B_BLOCK_DIM])
        log_a = -B_C * r * softplus[:, c0:c0 + B_BLOCK_DIM]
        a = jnp.exp(log_a)
        mult = jnp.sqrt(-jnp.tanh(log_a) * (a * a + 1.0))
        a_scr[:, c0:c0 + B_BLOCK_DIM] = a
        u_scr[:, c0:c0 + B_BLOCK_DIM] = mult * (ig * xcn)

    @pl.when(step == 0)
    def _():
        h_scr[...] = jnp.zeros_like(h_scr)

    n_groups = tile // SUBLANES
    row8 = lax.broadcasted_iota(jnp.int32, (SUBLANES, D_MODEL), 0)

    def body(gi, h):
        g = (n_groups - 1 - gi) if reverse else gi
        off = pl.multiple_of(g * SUBLANES, SUBLANES)
        a8 = a_scr[pl.ds(off, SUBLANES), :]
        u8 = u_scr[pl.ds(off, SUBLANES), :]
        for s in (1, 2, 4):
            if reverse:
                a_sh = pltpu.roll(a8, SUBLANES - s, 0)
                u_sh = pltpu.roll(u8, SUBLANES - s, 0)
                keep = row8 < SUBLANES - s
            else:
                a_sh = pltpu.roll(a8, s, 0)
                u_sh = pltpu.roll(u8, s, 0)
                keep = row8 >= s
            u8 = jnp.where(keep, a8 * u_sh + u8, u8)
            a8 = jnp.where(keep, a8 * a_sh, a8)
        hs = a8 * h + u8
        u_scr[pl.ds(off, SUBLANES), :] = hs
        return hs[0:1, :] if reverse else hs[SUBLANES - 1:SUBLANES, :]

    h_scr[...] = lax.fori_loop(0, n_groups, body, h_scr[...], unroll=4)

    if reverse:
        y = hf_ref[...] + u_scr[...]
        z = (y * _silu(gate_ref[...].astype(F32))).astype(BF16)
        x_new = x_ref[...] + _dot(z, wo_ref[...])
        out_ref[...] = x_new
        h_next = _rmsnorm(x_new, gnext_ref[...]).astype(BF16)
        for c in range(n_next):
            cols = slice(c * D_MODEL, (c + 1) * D_MODEL)
            pnext_ref[:, cols] = _dot(h_next, wnext_ref[:, cols]).astype(pnext_ref.dtype)
    else:
        out_ref[...] = u_scr[...]


def _rglru(proj, conv_w, conv_b, wg, b_r, b_i, lam, *, reverse, xc=None, hf=None, x=None, w_out=None,
           g_next=None, w_next=None, tile=512):
    bsz, seq, _ = proj.shape
    tile = min(tile, seq)
    n_tiles = seq // tile
    per = tile // BF16_ROWS
    last = seq // BF16_ROWS - 1

    def tix(i):
        return (n_tiles - 1 - i) if reverse else i

    main = pl.BlockSpec((None, tile, D_MODEL), lambda b, i: (b, tix(i), 0))
    vec = pl.BlockSpec((1, D_MODEL), lambda b, i: (0, 0))
    gate_w = pl.BlockSpec((B_BLOCKS, B_BLOCK_DIM, 2 * B_BLOCK_DIM), lambda b, i: (0, 0, 0))
    gate_args = [wg, b_r.reshape(1, D_MODEL), b_i.reshape(1, D_MODEL), lam.reshape(1, D_MODEL)]
    f32_out = jax.ShapeDtypeStruct((bsz, seq, D_MODEL), F32)
    n_next = 0
    if reverse:
        layer_next, width_next = w_next[1], w_next[0].shape[2]
        n_next = width_next // D_MODEL
        once = pl.Buffered(1)
        in_specs = [main, gate_w, vec, vec, vec,
                    main, pl.BlockSpec((None, tile, D_MODEL), lambda b, i: (b, tix(i), 1)), main,
                    pl.BlockSpec((D_MODEL, D_MODEL), lambda b, i: (0, 0), pipeline_mode=once), vec,
                    pl.BlockSpec((None, D_MODEL, width_next), lambda b, i: (layer_next, 0, 0),
                                 pipeline_mode=once)]
        args = [xc] + gate_args + [hf, proj, x, w_out, g_next.reshape(1, D_MODEL), w_next[0]]
        out_specs = [main, pl.BlockSpec((None, tile, width_next), lambda b, i: (b, tix(i), 0))]
        out_shape = [f32_out, jax.ShapeDtypeStruct((bsz, seq, width_next), BF16)]
    else:
        halo_p = pl.BlockSpec((None, BF16_ROWS, D_MODEL), lambda b, i: (b, jnp.maximum(i * per - 1, 0), 0))
        halo_n = pl.BlockSpec((None, BF16_ROWS, D_MODEL), lambda b, i: (b, jnp.minimum((i + 1) * per, last), 0))
        in_specs = [main, halo_p, halo_n, pl.BlockSpec((4, D_MODEL), lambda b, i: (0, 0)), vec,
                    gate_w, vec, vec, vec]
        args = [proj, proj, proj, conv_w, conv_b.reshape(1, D_MODEL)] + gate_args
        out_specs = [main, main]
        out_shape = [f32_out, f32_out]
    return pl.pallas_call(
        functools.partial(_rglru_kernel, tile=tile, n_tiles=n_tiles, reverse=reverse, n_next=n_next),
        grid=(bsz, n_tiles),
        in_specs=in_specs,
        out_specs=out_specs,
        out_shape=out_shape,
        scratch_shapes=[pltpu.VMEM((tile, D_MODEL), F32), pltpu.VMEM((tile, D_MODEL), F32),
                        pltpu.VMEM((1, D_MODEL), F32)],
        compiler_params=_params(("arbitrary", "arbitrary")),
        name="rglru_bwd_out" if reverse else "rglru_fwd",
    )(*args)


def _mixer_b(x, g, w_in, layer, conv_w, conv_b, w_r, b_r, w_i, b_i, lam, w_out, g_next, w_next):
    bsz, seq, _ = x.shape
    proj = _norm_proj(x, g, w_in, layer, [0, 1]).reshape(bsz, seq, 2 * D_MODEL)
    wg = [jnp.concatenate([w_r[d], w_i[d]], axis=-1).astype(BF16) for d in range(2)]
    hf, xc = _rglru(proj, conv_w, conv_b, wg[0], b_r[0], b_i[0], lam[0], reverse=False)
    return _rglru(proj, conv_w, conv_b, wg[1], b_r[1], b_i[1], lam[1], reverse=True,
                  xc=xc, hf=hf, x=x, w_out=w_out.astype(BF16), g_next=g_next, w_next=(w_next, layer))


def _split3(x):
    hi = x.astype(BF16)
    r1 = x - hi.astype(F32)
    mid = r1.astype(BF16)
    lo = (r1 - mid.astype(F32)).astype(BF16)
    return hi, mid, lo


def _hgrn_kernel(*refs, tile, reverse):
    if reverse:
        (q_ref, z_ref, v_ref, lb_ref, tri_ref, of_ref, gate_ref, x_ref, gn_ref, wo_ref,
         out_ref, st_scr, o_scr) = refs
    else:
        q_ref, z_ref, v_ref, lb_ref, tri_ref, out_ref, st_scr, o_scr = refs

    @pl.when(pl.program_id(1) == 0)
    def _():
        st_scr[...] = jnp.zeros_like(st_scr)

    n_blk = C_TILE // C_CHUNK
    order = list(reversed(range(n_blk))) if reverse else list(range(n_blk))
    mid_row = C_CHUNK // 2 - 1 if reverse else C_CHUNK // 2
    end_row = 0 if reverse else C_CHUNK - 1
    lb = lb_ref[...]
    one_m_lb = 1.0 - lb
    tri = tri_ref[...]

    n_sub = tile // C_TILE
    for sub in (reversed(range(n_sub)) if reverse else range(n_sub)):
        base = sub * C_TILE
        qt, kt, rho, blk_total = {}, {}, {}, {}
        for j in order:
            r0 = base + j * C_CHUNK
            z = z_ref[r0:r0 + C_CHUNK, :].astype(F32)
            half_t = 0.5 * jnp.tanh(0.5 * z)
            sig_pos = 0.5 + half_t
            sig_neg = 0.5 - half_t
            logf = jnp.log(lb + one_m_lb * sig_pos)
            b = _dot(tri, jnp.concatenate(_split3(logf), axis=0))
            b_mid = b[mid_row:mid_row + 1, :]
            qt[j] = q_ref[r0:r0 + C_CHUNK, :].astype(F32) * jnp.exp(b - b_mid)
            kt[j] = (one_m_lb * sig_neg) * jnp.exp(b_mid - b)
            rho[j] = b_mid
            blk_total[j] = b[end_row:end_row + 1, :]

        off = jnp.zeros_like(lb)
        for j in order:
            rho[j] = rho[j] + off
            off = off + blk_total[j]
        b_end = off
        decay = jnp.exp(b_end)

        qd = jnp.concatenate([(qt[j] * jnp.exp(rho[j])).astype(BF16) for j in range(n_blk)], axis=0)
        kd = jnp.concatenate([(kt[j] * jnp.exp(b_end - rho[j])).astype(BF16) for j in range(n_blk)], axis=0)

        zeros = jnp.zeros((C_CHUNK, D_MODEL), BF16)
        keys = {}
        for i in range(n_blk):
            parts = []
            for j in range(n_blk):
                if j == i:
                    parts.append(kt[j].astype(BF16))
                elif order.index(j) < order.index(i):
                    parts.append((kt[j] * jnp.exp(rho[i] - rho[j])).astype(BF16))
                else:
                    parts.append(zeros)
            keys[i] = jnp.concatenate(parts, axis=0)
        qt = {j: qt[j].astype(BF16) for j in qt}

        ti = lax.broadcasted_iota(jnp.int32, (C_TILE, C_TILE), 0)
        si = lax.broadcasted_iota(jnp.int32, (C_TILE, C_TILE), 1)
        visible = (si >= ti) if reverse else (si <= ti)
        zq = jnp.zeros((C_CHUNK, C_KDIM), BF16)
        rows = slice(base, base + C_TILE)
        for hd in range(C_HEADS):
            cs = slice(hd * C_KDIM, (hd + 1) * C_KDIM)
            q_diag = jnp.concatenate(
                [jnp.concatenate([qt[i][:, cs] if j == i else zq for j in range(n_blk)], axis=1)
                 for i in range(n_blk)], axis=0)
            k_cat = jnp.concatenate([keys[i][:, cs] for i in range(n_blk)], axis=1)
            att = jnp.where(visible, _dot_nt(q_diag, k_cat), 0.0).astype(BF16)
            st = st_scr[hd]
            o_scr[rows, cs] = _dot(att, v_ref[rows, cs]) + _dot_nt(qd[:, cs], st.astype(BF16))
            st_scr[hd] = decay[:, cs] * st + _dot_tn(v_ref[rows, cs], kd[:, cs])

    if reverse:
        gn = gn_ref[...]
        for hd in range(C_HEADS):
            c0 = hd * C_KDIM
            o = of_ref[:, c0:c0 + C_KDIM] + o_scr[:, c0:c0 + C_KDIM]
            o = o * lax.rsqrt(jnp.mean(o * o, axis=-1, keepdims=True) + EPS) * gn
            o_scr[:, c0:c0 + C_KDIM] = o
        z = (o_scr[...] * _silu(gate_ref[...].astype(F32))).astype(BF16)
        out_ref[...] = x_ref[...] + _dot(z, wo_ref[...])
    else:
        out_ref[...] = o_scr[...]


def _hgrn(proj, lb, *, reverse, of=None, x=None, gn=None, w_out=None, tile=512):
    bsz, seq, _ = proj.shape
    tile = min(tile, seq)
    n_tiles = seq // tile

    def tix(i):
        return (n_tiles - 1 - i) if reverse else i

    def col(j):
        return pl.BlockSpec((None, tile, D_MODEL), lambda b, i: (b, tix(i), j))

    ti = lax.broadcasted_iota(jnp.int32, (C_CHUNK, C_CHUNK), 0)
    si = lax.broadcasted_iota(jnp.int32, (C_CHUNK, C_CHUNK), 1)
    tri = ((si >= ti) if reverse else (si <= ti)).astype(BF16)
    tri = jnp.concatenate([tri, tri, tri], axis=1)
    in_specs = [col(0), col(2 if reverse else 1), col(3),
                pl.BlockSpec((1, D_MODEL), lambda b, i: (0, 0)),
                pl.BlockSpec((C_CHUNK, 3 * C_CHUNK), lambda b, i: (0, 0))]
    args = [proj, proj, proj, lb.reshape(1, D_MODEL), tri]
    if reverse:
        in_specs += [col(0), col(4), col(0),
                     pl.BlockSpec((1, C_KDIM), lambda b, i: (0, 0)),
                     pl.BlockSpec((D_MODEL, D_MODEL), lambda b, i: (0, 0))]
        args += [of, proj, x, gn.reshape(1, C_KDIM), w_out]
    return pl.pallas_call(
        functools.partial(_hgrn_kernel, tile=tile, reverse=reverse),
        grid=(bsz, n_tiles),
        in_specs=in_specs,
        out_specs=col(0),
        out_shape=jax.ShapeDtypeStruct((bsz, seq, D_MODEL), F32),
        scratch_shapes=[pltpu.VMEM((C_HEADS, C_KDIM, C_KDIM), F32), pltpu.VMEM((tile, D_MODEL), F32)],
        compiler_params=_params(("arbitrary", "arbitrary")),
        name="hgrn_bwd_out" if reverse else "hgrn_fwd",
    )(*args)


def _mixer_c(x, proj, lb, gn, w_out):
    of = _hgrn(proj, lb[0], reverse=False)
    return _hgrn(proj, lb[1], reverse=True, of=of, x=x, gn=gn, w_out=w_out.astype(BF16))


def _trunk(x, rope_base, norm_g, final_g, a_w_in, a_w_out, b_w_in, b_conv_w, b_conv_b, b_w_r, b_b_r, b_w_i, b_b_i,
           b_lambda, b_w_out, c_w_in, c_lower_bounds, c_gnorm_g, c_w_out):
    depth = norm_g.shape[0]
    sm = jax.nn.softmax(c_lower_bounds.astype(F32), axis=0)
    lbs = jnp.cumsum(sm, axis=0) - sm[0]
    if depth % N_MIXERS != 1 or depth < 1:
        raise NotImplementedError("trunk depth must end on an attention layer")
    proj_c = None
    for layer in range(depth):
        kind = layer % N_MIXERS
        j = layer // N_MIXERS
        if kind == 0:
            x = _mixer_a(x, norm_g[layer], a_w_in, j, a_w_out[j], final_g if layer == depth - 1 else None,
                         rope_base)
        elif kind == 1:
            x, proj_c = _mixer_b(x, norm_g[layer], b_w_in, j, b_conv_w[j], b_conv_b[j], b_w_r[j], b_b_r[j],
                                 b_w_i[j], b_b_i[j], b_lambda[j], b_w_out[j], norm_g[layer + 1], c_w_in)
        else:
            x = _mixer_c(x, proj_c, lbs[layer], c_gnorm_g[j], c_w_out[j])
    return x


def kernel(x_prompt, x_sample, norm_g, final_g, a_w_in, a_w_out, b_w_in, b_conv_w, b_conv_b, b_w_r, b_b_r,
           b_w_i, b_b_i, b_lambda, b_w_out, c_w_in, c_lower_bounds, c_gnorm_g, c_w_out):
    weights = (norm_g, final_g, a_w_in.astype(BF16), a_w_out, b_w_in.astype(BF16), b_conv_w, b_conv_b,
               b_w_r, b_b_r, b_w_i, b_b_i, b_lambda, b_w_out, c_w_in.astype(BF16), c_lower_bounds,
               c_gnorm_g, c_w_out)
    rope_base = _rope_base(max(x_prompt.shape[1], x_sample.shape[1]))
    return (_trunk(x_prompt, rope_base, *weights), _trunk(x_sample, rope_base, *weights))
```
